```python
import math, functools
import jax, jax.numpy as jnp
from jax import lax
import numpy as np

D_MODEL = 1024
BATCH = 8
SEQ = 2048
DEPTH = 1
DEC_BATCH = 128
DEC_SEQ = 1
PAST_LEN = 16384
PAGE_SIZE = 128

MLA_HEADS = 8
MLA_NOPE = 64
MLA_ROPE = 32
MLA_VDIM = 64
Q_LORA = 384
KV_LORA = 256
HG_HEADS = 4
HG_DK = 128
HG_DV = 128
HG_CHUNK = 32
MLA_WIDTH = MLA_HEADS * MLA_VDIM
HG_QK = HG_HEADS * HG_DK
HG_WIDTH = HG_HEADS * HG_DV
MIX_WIDTH = MLA_WIDTH + HG_WIDTH
OFF_CKV = Q_LORA
OFF_KR = OFF_CKV + KV_LORA
OFF_HG = OFF_KR + MLA_ROPE
IN_COLS = OFF_HG + 2 * HG_QK + 2 * HG_WIDTH
MEM_TOKENS = 256
MEM_HEADS = 4
MEM_HDIM = 128
D_FF = 2816
CONV_W = 3

Q_BLOCK = 128
ROPE_BASE = 10000.0
EPS = 1e-6

kernel_name = 'hybrid_mla_hgrn2_decode_step'


def rms_norm(x, g):
    xf = x.astype(jnp.float32)
    y = xf * lax.rsqrt(jnp.mean(xf * xf, axis=-1, keepdims=True) + EPS)
    return (y * g.astype(jnp.float32)).astype(x.dtype)


def rope(x, pos):
    half = x.shape[-1] // 2
    inv = ROPE_BASE ** (-jnp.arange(half, dtype=jnp.float32) / half)
    ang = pos[:, None] * inv[None, :]
    ang = ang.reshape(ang.shape[:1] + (1,) * (x.ndim - 3) + ang.shape[1:])
    cos, sin = jnp.cos(ang), jnp.sin(ang)
    xf = x.astype(jnp.float32)
    x1, x2 = xf[..., :half], xf[..., half:]
    return jnp.concatenate([x1 * cos - x2 * sin, x1 * sin + x2 * cos], axis=-1).astype(x.dtype)


def mla_prompt_attention(q_nope, q_rope, c_kv, k_rope, w_uk, w_uv):
    B, T = c_kv.shape[:2]
    k_nope = jnp.einsum('btl,lhn->bthn', c_kv, w_uk)
    v = jnp.einsum('btl,lhv->bthv', c_kv, w_uv)
    scale = (MLA_NOPE + MLA_ROPE) ** -0.5
    qb = math.gcd(T, Q_BLOCK)
    key_pos = jnp.arange(T)

    def block(i):
        s0 = i * qb
        qn = lax.dynamic_slice_in_dim(q_nope, s0, qb, axis=1)
        qr = lax.dynamic_slice_in_dim(q_rope, s0, qb, axis=1)
        s = (jnp.einsum('bqhn,bkhn->bhqk', qn, k_nope)
             + jnp.einsum('bqhr,bkr->bhqk', qr, k_rope)).astype(jnp.float32) * scale
        q_pos = s0 + jnp.arange(qb)
        s = jnp.where(key_pos[None, :] <= q_pos[:, None], s, -jnp.inf)
        p = jax.nn.softmax(s, axis=-1).astype(v.dtype)
        return jnp.einsum('bhqk,bkhv->bqhv', p, v)

    out = lax.map(block, jnp.arange(T // qb))
    return out.transpose(1, 0, 2, 3, 4).reshape(B, T, MLA_HEADS, MLA_VDIM)


def mla_sample_attention(q_nope, q_rope, c_kv, k_rope, ckv_past, krope_past, w_uk, w_uv):
    T = c_kv.shape[1]
    P = ckv_past.shape[1]
    scale = (MLA_NOPE + MLA_ROPE) ** -0.5
    q_lat = jnp.einsum('bthn,lhn->bthl', q_nope, w_uk)
    s_past = (jnp.einsum('bthl,bpl->bhtp', q_lat, ckv_past)
              + jnp.einsum('bthr,bpr->bhtp', q_rope, krope_past))
    s_new = (jnp.einsum('bthl,bsl->bhts', q_lat, c_kv)
             + jnp.einsum('bthr,bsr->bhts', q_rope, k_rope))
    causal = jnp.tril(jnp.ones((T, T), dtype=bool))
    s_new = jnp.where(causal, s_new.astype(jnp.float32), -jnp.inf)
    s = jnp.concatenate([s_past.astype(jnp.float32), s_new], axis=-1) * scale
    p = jax.nn.softmax(s, axis=-1).astype(c_kv.dtype)
    lat = (jnp.einsum('bhtp,bpl->bthl', p[..., :P], ckv_past)
           + jnp.einsum('bhts,bsl->bthl', p[..., P:], c_kv))
    return jnp.einsum('bthl,lhv->bthv', lat, w_uv)


def hgrn2_scan(q, k, v, log_f, s0):
    B, T, H, DK = q.shape
    DV = v.shape[-1]
    c = math.gcd(T, HG_CHUNK)
    n = T // c

    def to_chunks(a):
        return a.astype(jnp.float32).reshape(B, n, c, H, a.shape[-1]).transpose(1, 0, 3, 2, 4)

    qc, kc, vc, gc = to_chunks(q), to_chunks(k), to_chunks(v), to_chunks(log_f)
    mask = jnp.tril(jnp.ones((c, c), dtype=bool))

    def step(S, inp):
        qi, ki, vi, gi = inp
        b = jnp.cumsum(gi, axis=-2)
        q_t = qi * jnp.exp(b)
        a = jnp.einsum('bhtd,bhsd->bhts', q_t, ki * jnp.exp(-b))
        a = jnp.where(mask, a, 0.0)
        o = jnp.einsum('bhts,bhsv->bhtv', a, vi) + jnp.einsum('bhtd,bhdv->bhtv', q_t, S)
        decay = jnp.exp(b[..., -1, :])
        S = decay[..., None] * S + jnp.einsum('bhsd,bhsv->bhdv', ki * jnp.exp(b[..., -1:, :] - b), vi)
        return S, o

    S, o = lax.scan(step, s0.astype(jnp.float32), (qc, kc, vc, gc))
    o = o.transpose(1, 0, 3, 2, 4).reshape(B, T, H, DV)
    return o, S


def hgrn2_group(zh, lb, g_norm, s0):
    B, T, _ = zh.shape
    q = zh[..., :HG_QK].reshape(B, T, HG_HEADS, HG_DK)
    f_raw = zh[..., HG_QK:2 * HG_QK].astype(jnp.float32)
    i_in = zh[..., 2 * HG_QK:2 * HG_QK + HG_WIDTH].reshape(B, T, HG_HEADS, HG_DV)
    gate = zh[..., 2 * HG_QK + HG_WIDTH:]
    f = lb + (1.0 - lb) * jax.nn.sigmoid(f_raw)
    k = (1.0 - f).reshape(B, T, HG_HEADS, HG_DK)
    log_f = jnp.log(f).reshape(B, T, HG_HEADS, HG_DK)
    o, S = hgrn2_scan(q, k, i_in, log_f, s0)
    o = rms_norm(o, g_norm).reshape(B, T, HG_WIDTH) * jax.nn.silu(gate.astype(jnp.float32))
    return o.astype(zh.dtype), S


def memory_kv(mem, g_mem, w_mk, w_mv):
    m = rms_norm(mem, g_mem)
    k = jnp.einsum('bmd,dhe->bmhe', m, w_mk)
    v = jnp.einsum('bmd,dhe->bmhe', m, w_mv)
    return k, v


def memory_attention(h, mem_k, mem_v, w_mq, w_mo):
    q = jnp.einsum('btd,dhe->bthe', h, w_mq)
    s = jnp.einsum('bthe,bmhe->bhtm', q, mem_k).astype(jnp.float32) * (MEM_HDIM ** -0.5)
    p = jax.nn.softmax(s, axis=-1).astype(mem_v.dtype)
    o = jnp.einsum('bhtm,bmhe->bthe', p, mem_v)
    return jnp.einsum('bthe,hed->btd', o, w_mo)


def conv_ffn(h, prefix, w_up, conv_w, conv_b, w_down):
    T = h.shape[1]
    ab = jnp.einsum('btd,df->btf', h, w_up)
    a, gate_in = ab[..., :D_FF], ab[..., D_FF:]
    ext = jnp.concatenate([prefix.astype(a.dtype), a], axis=1)
    conv = conv_b
    for tap in range(CONV_W):
        conv = conv + ext[:, tap:tap + T] * conv_w[tap]
    y = jnp.einsum('btf,fd->btd', jax.nn.silu(conv) * gate_in, w_down)
    return y, ext[:, T:]


def trunk_layer(x, start, mla_attend, s0, conv_prefix, mem_k, mem_v, p):
    B, T, _ = x.shape
    pos = jnp.arange(T, dtype=jnp.float32) + start
    h = rms_norm(x, p['norm_mix'])
    z = jnp.einsum('btd,dc->btc', h, p['w_in'])
    c_q = rms_norm(z[..., :OFF_CKV], p['norm_q'])
    c_kv = rms_norm(z[..., OFF_CKV:OFF_KR], p['norm_kv'])
    k_rope = rope(z[..., OFF_KR:OFF_HG], pos)
    q = jnp.einsum('btc,chd->bthd', c_q, p['w_uq'])
    q_nope, q_rope = q[..., :MLA_NOPE], rope(q[..., MLA_NOPE:], pos)
    o_mla = mla_attend(q_nope, q_rope, c_kv, k_rope).reshape(B, T, MLA_WIDTH)
    o_hg, s_new = hgrn2_group(z[..., OFF_HG:], p['lb'], p['norm_hg'], s0)
    x = x + jnp.einsum('btc,cd->btd', jnp.concatenate([o_mla, o_hg], axis=-1), p['w_out'])
    x = x + memory_attention(rms_norm(x, p['norm_memx']), mem_k, mem_v, p['w_mq'], p['w_mo'])
    y, conv_tail = conv_ffn(rms_norm(x, p['norm_ffn']), conv_prefix, p['w_up'], p['conv_w'], p['conv_b'], p['w_down'])
    x = x + y
    return x, c_kv, k_rope, s_new, conv_tail


def setup_inputs(seed: int = 0) -> dict:
    key = jax.random.key(seed)
    ks = iter(jax.random.split(key, 48))

    def nrm(shape, scale):
        return jax.random.normal(next(ks), shape, jnp.float32) * scale

    def gain(shape):
        return 1.0 + 0.01 * jax.random.normal(next(ks), shape, jnp.float32)

    n_pages = PAST_LEN // PAGE_SIZE
    n_used = DEC_BATCH * n_pages
    n_pool = n_used + max(1, n_used // 4)
    page_table = jax.random.permutation(next(ks), n_pool)[:n_used].reshape(DEC_BATCH, n_pages).astype(jnp.int32)
    return {
        'x_prompt': nrm((BATCH, SEQ, D_MODEL), 1.0),
        'x_sample': nrm((DEC_BATCH, DEC_SEQ, D_MODEL), 1.0),
        'mem_prompt': nrm((BATCH, MEM_TOKENS, D_MODEL), 1.0),
        'cache_ckv': nrm((DEPTH, n_pool, PAGE_SIZE, KV_LORA), 1.0),
        'cache_krope': nrm((DEPTH, n_pool, PAGE_SIZE, MLA_ROPE), 1.0),
        'page_table': page_table,
        'cache_mem_k': nrm((DEPTH, DEC_BATCH, MEM_TOKENS, MEM_HEADS, MEM_HDIM), 1.0),
        'cache_mem_v': nrm((DEPTH, DEC_BATCH, MEM_TOKENS, MEM_HEADS, MEM_HDIM), 1.0),
        'state_hgrn': nrm((DEPTH, DEC_BATCH, HG_HEADS, HG_DK, HG_DV), 0.5),
        'state_conv': nrm((DEPTH, DEC_BATCH, CONV_W - 1, D_FF), 1.0),
        'norm_mix': gain((DEPTH, D_MODEL)),
        'w_in': nrm((DEPTH, D_MODEL, IN_COLS), D_MODEL ** -0.5),
        'norm_q': gain((DEPTH, Q_LORA)),
        'norm_kv': gain((DEPTH, KV_LORA)),
        'w_uq': nrm((DEPTH, Q_LORA, MLA_HEADS, MLA_NOPE + MLA_ROPE), Q_LORA ** -0.5),
        'w_uk': nrm((DEPTH, KV_LORA, MLA_HEADS, MLA_NOPE), KV_LORA ** -0.5),
        'w_uv': nrm((DEPTH, KV_LORA, MLA_HEADS, MLA_VDIM), KV_LORA ** -0.5),
        'hg_lb_raw': nrm((DEPTH + 1, HG_QK), 0.1),
        'norm_hg': gain((DEPTH, HG_DV)),
        'w_out': nrm((DEPTH, MIX_WIDTH, D_MODEL), MIX_WIDTH ** -0.5),
        'norm_memx': gain((DEPTH, D_MODEL)),
        'norm_mem': gain((DEPTH, D_MODEL)),
        'w_mq': nrm((DEPTH, D_MODEL, MEM_HEADS, MEM_HDIM), D_MODEL ** -0.5),
        'w_mk': nrm((DEPTH, D_MODEL, MEM_HEADS, MEM_HDIM), D_MODEL ** -0.5),
        'w_mv': nrm((DEPTH, D_MODEL, MEM_HEADS, MEM_HDIM), D_MODEL ** -0.5),
        'w_mo': nrm((DEPTH, MEM_HEADS, MEM_HDIM, D_MODEL), (MEM_HEADS * MEM_HDIM) ** -0.5),
        'norm_ffn': gain((DEPTH, D_MODEL)),
        'w_up': nrm((DEPTH, D_MODEL, 2 * D_FF), D_MODEL ** -0.5),
        'conv_w': nrm((DEPTH, CONV_W, D_FF), CONV_W ** -0.5),
        'conv_b': nrm((DEPTH, D_FF), 0.01),
        'w_down': nrm((DEPTH, D_FF, D_MODEL), D_FF ** -0.5),
        'norm_final': gain((D_MODEL,)),
    }


def reference(x_prompt, x_sample, mem_prompt, cache_ckv, cache_krope, page_table, cache_mem_k, cache_mem_v,
              state_hgrn, state_conv, norm_mix, w_in, norm_q, norm_kv, w_uq, w_uk, w_uv, hg_lb_raw, norm_hg,
              w_out, norm_memx, norm_mem, w_mq, w_mk, w_mv, w_mo, norm_ffn, w_up, conv_w, conv_b, w_down,
              norm_final):
    lb_all = jnp.cumsum(jax.nn.softmax(hg_lb_raw.astype(jnp.float32), axis=0), axis=0)
    past_len = page_table.shape[1] * cache_ckv.shape[2]
    bp, db = x_prompt.shape[0], x_sample.shape[0]
    xp, xs = x_prompt, x_sample
    ckv_p_l, kr_p_l, mk_p_l, mv_p_l, s_p_l, cv_p_l = [], [], [], [], [], []
    ckv_s_l, kr_s_l, s_s_l, cv_s_l = [], [], [], []
    for l in range(DEPTH):
        p = {'norm_mix': norm_mix[l], 'w_in': w_in[l], 'norm_q': norm_q[l], 'norm_kv': norm_kv[l],
             'w_uq': w_uq[l], 'lb': lb_all[l], 'norm_hg': norm_hg[l], 'w_out': w_out[l],
             'norm_memx': norm_memx[l], 'w_mq': w_mq[l], 'w_mo': w_mo[l], 'norm_ffn': norm_ffn[l],
             'w_up': w_up[l], 'conv_w': conv_w[l], 'conv_b': conv_b[l], 'w_down': w_down[l]}
        mk_p, mv_p = memory_kv(mem_prompt, norm_mem[l], w_mk[l], w_mv[l])
        attend_p = functools.partial(mla_prompt_attention, w_uk=w_uk[l], w_uv=w_uv[l])
        s0_p = jnp.zeros((bp, HG_HEADS, HG_DK, HG_DV), jnp.float32)
        pre_p = jnp.zeros((bp, CONV_W - 1, D_FF), x_prompt.dtype)
        xp, ckv_p, kr_p, s_p, cv_p = trunk_layer(xp, 0, attend_p, s0_p, pre_p, mk_p, mv_p, p)
        ckv_past = cache_ckv[l, page_table].reshape(db, past_len, KV_LORA)
        kr_past = cache_krope[l, page_table].reshape(db, past_len, MLA_ROPE)
        attend_s = functools.partial(mla_sample_attention, ckv_past=ckv_past, krope_past=kr_past,
                                     w_uk=w_uk[l], w_uv=w_uv[l])
        xs, ckv_s, kr_s, s_s, cv_s = trunk_layer(xs, past_len, attend_s, state_hgrn[l], state_conv[l],
                                                 cache_mem_k[l], cache_mem_v[l], p)
        ckv_p_l.append(ckv_p)
        kr_p_l.append(kr_p)
        mk_p_l.append(mk_p)
        mv_p_l.append(mv_p)
        s_p_l.append(s_p.astype(x_prompt.dtype))
        cv_p_l.append(cv_p)
        ckv_s_l.append(ckv_s)
        kr_s_l.append(kr_s)
        s_s_l.append(s_s.astype(state_hgrn.dtype))
        cv_s_l.append(cv_s.astype(state_conv.dtype))
    y_prompt = rms_norm(xp, norm_final)
    y_sample = rms_norm(xs, norm_final)
    return (y_prompt, y_sample,
            jnp.stack(ckv_p_l), jnp.stack(kr_p_l), jnp.stack(mk_p_l), jnp.stack(mv_p_l),
            jnp.stack(s_p_l), jnp.stack(cv_p_l),
            jnp.stack(ckv_s_l), jnp.stack(kr_s_l), jnp.stack(s_s_l), jnp.stack(cv_s_l))
```

```python
import functools

import jax
import jax.numpy as jnp
from jax import lax
from jax.experimental import pallas as pl
from jax.experimental.pallas import tpu as pltpu

F32 = jnp.float32
BF16 = jnp.bfloat16

D_MODEL = 1024
MLA_HEADS = 8
MLA_NOPE = 64
MLA_ROPE = 32
MLA_VDIM = 64
Q_LORA = 384
KV_LORA = 256
HG_HEADS = 4
HG_DK = 128
HG_DV = 128
HG_CHUNK = 32
HG_QK = HG_HEADS * HG_DK
HG_WIDTH = HG_HEADS * HG_DV
MLA_WIDTH = MLA_HEADS * MLA_VDIM
MEM_HEADS = 4
MEM_HDIM = 128
MEM_WIDTH = MEM_HEADS * MEM_HDIM
D_FF = 2816
CONV_W = 3
ROPE_BASE = 10000.0
EPS = 1e-6

LANE = 128
SUBLANE = 8
HEAD_PAD = LANE
ROPE_HALF = MLA_ROPE // 2
OFF_CKV = Q_LORA
OFF_KRT = OFF_CKV + KV_LORA
OFF_HGP = OFF_KRT + LANE
IN_COLS_PAD = OFF_HGP + 2 * HG_QK + 2 * HG_WIDTH
VMEM_LIMIT = 56 * 1024 * 1024
ROW_TILE = 512
FFN_CHUNK = D_FF // 2
PAGES_PER_CHUNK = 8
NEG_BIG = -1e30


def _rms(x, g):
    return x * lax.rsqrt(jnp.mean(x * x, axis=-1, keepdims=True) + EPS) * g


def _dot(a, b):
    return jnp.dot(a, b, preferred_element_type=F32)


def _dot_nt(a, b):
    return lax.dot_general(a, b, (((1,), (1,)), ((), ())), preferred_element_type=F32)


def _dot_tn(a, b):
    return lax.dot_general(a, b, (((0,), (0,)), ((), ())), preferred_element_type=F32)


def _silu(x):
    return x * jax.nn.sigmoid(x)


def _params(*sem):
    return pltpu.CompilerParams(dimension_semantics=sem, vmem_limit_bytes=VMEM_LIMIT)


def _const_spec(shape):
    nd = len(shape)
    return pl.BlockSpec(shape, lambda *_: (0,) * nd, pipeline_mode=pl.Buffered(1))


def _memkv_kernel(mem_ref, g_ref, wk_ref, wv_ref, k_ref, v_ref, kb_ref, vb_ref):
    m = _rms(mem_ref[...], g_ref[...]).astype(BF16)
    k = _dot(m, wk_ref[...])
    v = _dot(m, wv_ref[...])
    k_ref[...] = k
    v_ref[...] = v
    kb_ref[...] = k.astype(BF16)
    vb_ref[...] = v.astype(BF16)


def _memkv(mem, g, wk, wv):
    rows = mem.shape[0]
    tm = min(ROW_TILE, rows)
    row = lambda w: pl.BlockSpec((tm, w), lambda i: (i, 0))
    return pl.pallas_call(
        _memkv_kernel,
        grid=(rows // tm,),
        in_specs=[row(D_MODEL), _const_spec((1, D_MODEL)), _const_spec((D_MODEL, MEM_WIDTH)),
                  _const_spec((D_MODEL, MEM_WIDTH))],
        out_specs=[row(MEM_WIDTH)] * 4,
        out_shape=[jax.ShapeDtypeStruct((rows, MEM_WIDTH), F32)] * 2
        + [jax.ShapeDtypeStruct((rows, MEM_WIDTH), BF16)] * 2,
        compiler_params=_params("parallel"),
        name="memkv",
    )(mem, g, wk, wv)


def _mix_in_kernel(x_ref, g_ref, win_ref, gq_ref, gkv_ref, wuq_ref, wuk_ref, wuv_ref, cos_ref, sin_ref,
                   q_ref, k_ref, v_ref, ckv_ref, kr_ref, zh_ref):
    h = _rms(x_ref[...], g_ref[...]).astype(BF16)
    z = _dot(h, win_ref[...])
    c_q = _rms(z[:, :OFF_CKV], gq_ref[...]).astype(BF16)
    c_kv = _rms(z[:, OFF_CKV:OFF_KRT], gkv_ref[...])
    ckv_ref[...] = c_kv
    c_kvb = c_kv.astype(BF16)
    zh_ref[...] = z[:, OFF_HGP:]
    cos = cos_ref[...]
    sin = sin_ref[...]
    zkr = z[:, OFF_KRT:OFF_HGP]
    kr = zkr * cos + pltpu.roll(zkr, LANE - MLA_ROPE, axis=1) * sin
    kr_ref[...] = kr
    scale = (MLA_NOPE + MLA_ROPE) ** -0.5
    cos_q = cos * scale
    sin_q = sin * scale
    q = _dot(c_q, wuq_ref[...])
    kn = _dot(c_kvb, wuk_ref[...])
    for hd in range(MLA_HEADS):
        sl = slice(hd * HEAD_PAD, (hd + 1) * HEAD_PAD)
        qh = q[:, sl]
        q_ref[:, sl] = (qh * cos_q + pltpu.roll(qh, LANE - MLA_ROPE, axis=1) * sin_q).astype(BF16)
        k_ref[:, sl] = (kn[:, sl] + kr).astype(BF16)
    v_ref[...] = _dot(c_kvb, wuv_ref[...]).astype(BF16)


def _mix_in(x, w, cos_t, sin_t, tm):
    rows = x.shape[0]
    t_tiles = cos_t.shape[0] // tm
    row = lambda wd: pl.BlockSpec((tm, wd), lambda i: (i, 0))
    tab = pl.BlockSpec((tm, LANE), lambda i: (i % t_tiles, 0))
    hq = MLA_HEADS * HEAD_PAD
    return pl.pallas_call(
        _mix_in_kernel,
        grid=(rows // tm,),
        in_specs=[row(D_MODEL), _const_spec((1, D_MODEL)), _const_spec((D_MODEL, IN_COLS_PAD)),
                  _const_spec((1, Q_LORA)), _const_spec((1, KV_LORA)), _const_spec((Q_LORA, hq)),
                  _const_spec((KV_LORA, hq)), _const_spec((KV_LORA, MLA_WIDTH)), tab, tab],
        out_specs=[row(hq), row(hq), row(MLA_WIDTH), row(KV_LORA), row(LANE), row(4 * HG_QK)],
        out_shape=[jax.ShapeDtypeStruct((rows, hq), BF16), jax.ShapeDtypeStruct((rows, hq), BF16),
                   jax.ShapeDtypeStruct((rows, MLA_WIDTH), BF16), jax.ShapeDtypeStruct((rows, KV_LORA), F32),
                   jax.ShapeDtypeStruct((rows, LANE), F32), jax.ShapeDtypeStruct((rows, 4 * HG_QK), F32)],
        compiler_params=_params("parallel"),
        name="mix_in",
    )(x, w["norm_mix"], w["w_in"], w["norm_q"], w["norm_kv"], w["w_uq"], w["w_uk"], w["w_uv"], cos_t, sin_t)


HEADS_PER_STEP = 2


def _mla_prompt_kernel(q_ref, k_ref, v_ref, o_ref, *, bq):
    qi = pl.program_id(2)
    outs = []
    for hh in range(HEADS_PER_STEP):
        ksl = slice(hh * HEAD_PAD, (hh + 1) * HEAD_PAD)
        vsl = slice(hh * MLA_VDIM, (hh + 1) * MLA_VDIM)
        q = q_ref[:, ksl]

        def block(j, carry, masked):
            m, l, acc = carry
            rows = pl.ds(pl.multiple_of(j * bq, bq), bq)
            s = _dot_nt(q, k_ref[rows, ksl])
            if masked:
                r = lax.broadcasted_iota(jnp.int32, (bq, bq), 0)
                c = lax.broadcasted_iota(jnp.int32, (bq, bq), 1)
                s = jnp.where(c <= r, s, NEG_BIG)
            m_new = jnp.maximum(m, jnp.max(s, axis=-1, keepdims=True))
            alpha = jnp.exp(m - m_new)
            p = jnp.exp(s - m_new)
            l = alpha * l + jnp.sum(p, axis=-1, keepdims=True)
            acc = alpha * acc + _dot(p.astype(BF16), v_ref[rows, vsl])
            return m_new, l, acc

        init = (jnp.full((bq, 1), NEG_BIG, F32), jnp.zeros((bq, 1), F32), jnp.zeros((bq, MLA_VDIM), F32))
        carry = lax.fori_loop(0, qi, functools.partial(block, masked=False), init)
        m, l, acc = block(qi, carry, True)
        outs.append(acc / l)
    o_ref[...] = jnp.concatenate(outs, axis=-1).astype(o_ref.dtype)


def _mla_prompt(q, k, v, bq):
    b, t, _ = q.shape
    hp = HEADS_PER_STEP
    return pl.pallas_call(
        functools.partial(_mla_prompt_kernel, bq=bq),
        grid=(b, MLA_HEADS // hp, t // bq),
        in_specs=[pl.BlockSpec((None, bq, hp * HEAD_PAD), lambda bi, hi, qi: (bi, qi, hi)),
                  pl.BlockSpec((None, t, hp * HEAD_PAD), lambda bi, hi, qi: (bi, 0, hi)),
                  pl.BlockSpec((None, t, hp * MLA_VDIM), lambda bi, hi, qi: (bi, 0, hi))],
        out_specs=pl.BlockSpec((None, bq, hp * MLA_VDIM), lambda bi, hi, qi: (bi, qi, hi)),
        out_shape=jax.ShapeDtypeStruct((b, t, MLA_WIDTH), BF16),
        compiler_params=_params("parallel", "parallel", "arbitrary"),
        name="mla_prompt",
    )(q, k, v)


def _lower_bound(raw):
    e = jnp.exp(raw - jnp.max(raw, axis=0, keepdims=True))
    return e[0:1] / jnp.sum(e, axis=0, keepdims=True)


def _hgrn_prompt_kernel(q_ref, f_ref, i_ref, gate_ref, lbraw_ref, gn_ref, o_ref, s_ref, *, t):
    c = HG_CHUNK
    lb = _lower_bound(lbraw_ref[...])
    gn = gn_ref[...]
    r = lax.broadcasted_iota(jnp.int32, (c, c), 0)
    cc = lax.broadcasted_iota(jnp.int32, (c, c), 1)
    tri = (cc <= r).astype(F32)

    def body(ci, st):
        rows = pl.ds(pl.multiple_of(ci * c, c), c)
        q = q_ref[rows, :]
        v = i_ref[rows, :]
        f = lb + (1.0 - lb) * jax.nn.sigmoid(f_ref[rows, :])
        k = 1.0 - f
        bcum = jnp.dot(tri, jnp.log(f), precision=lax.Precision.HIGHEST, preferred_element_type=F32)
        q_t = (q * jnp.exp(bcum)).astype(BF16)
        k_t = (k * jnp.exp(-bcum)).astype(BF16)
        a = jnp.where(cc <= r, _dot_nt(q_t, k_t), 0.0)
        vb = v.astype(BF16)
        o = _dot(a.astype(BF16), vb) + _dot_nt(q_t, st.astype(BF16))
        blast = bcum[c - 1:c, :]
        k_s = (k * jnp.exp(blast - bcum)).astype(BF16)
        st = st * jnp.exp(blast) + _dot_tn(vb, k_s)
        o_ref[rows, :] = (_rms(o, gn) * _silu(gate_ref[rows, :])).astype(o_ref.dtype)
        return st

    st = lax.fori_loop(0, t // c, body, jnp.zeros((HG_DV, HG_DK), F32))
    s_ref[...] = st.T


def _hgrn_prompt(zh, lb_raw, gn):
    b, t, _ = zh.shape
    col = lambda off: pl.BlockSpec((None, t, HG_DK), lambda bi, hi: (bi, 0, off + hi))
    return pl.pallas_call(
        functools.partial(_hgrn_prompt_kernel, t=t),
        grid=(b, HG_HEADS),
        in_specs=[col(0), col(HG_HEADS), col(2 * HG_HEADS), col(3 * HG_HEADS),
                  pl.BlockSpec((lb_raw.shape[0], HG_DK), lambda bi, hi: (0, hi)), _const_spec((1, HG_DV))],
        out_specs=[pl.BlockSpec((None, t, HG_DV), lambda bi, hi: (bi, 0, hi)),
                   pl.BlockSpec((None, None, HG_DK, HG_DV), lambda bi, hi: (bi, hi, 0, 0))],
        out_shape=[jax.ShapeDtypeStruct((b, t, HG_WIDTH), BF16),
                   jax.ShapeDtypeStruct((b, HG_HEADS, HG_DK, HG_DV), F32)],
        compiler_params=_params("parallel", "parallel"),
        name="hgrn_prompt",
    )(zh, zh, zh, zh, lb_raw, gn)


def _qlat_kernel(q_ref, wukt_ref, o_ref):
    for hd in range(MLA_HEADS):
        o_ref[hd] = _dot(q_ref[:, hd * HEAD_PAD:(hd + 1) * HEAD_PAD], wukt_ref[hd])


def _qlat(q_pad, wukt):
    n = q_pad.shape[0]
    return pl.pallas_call(
        _qlat_kernel,
        out_shape=jax.ShapeDtypeStruct((MLA_HEADS, n, KV_LORA), F32),
        compiler_params=pltpu.CompilerParams(vmem_limit_bytes=VMEM_LIMIT),
        name="qlat",
    )(q_pad, wukt)


def _paged_kernel(pt_ref, qlat_ref, qrope_ref, ckvn_ref, krn_ref, ckv_hbm, kr_hbm, lat_ref,
                  kvbuf, krbuf, sem, *, n_pages):
    b = pl.program_id(0)
    nb = pl.num_programs(0)
    pc = PAGES_PER_CHUNK
    n_chunks = n_pages // pc

    def copies(bb, ch, slot):
        out = []
        for p in range(pc):
            pg = pt_ref[bb, ch * pc + p]
            out.append(pltpu.make_async_copy(ckv_hbm.at[pg], kvbuf.at[slot, p], sem.at[0, slot]))
            out.append(pltpu.make_async_copy(kr_hbm.at[pg], krbuf.at[slot, p], sem.at[1, slot]))
        return out

    def start(bb, ch, slot):
        for cp in copies(bb, ch, slot):
            cp.start()

    @pl.when(b == 0)
    def _():
        start(0, 0, 0)

    qlat = qlat_ref[...]
    qrope = qrope_ref[...]
    qlat_b = qlat.astype(BF16)
    qrope_b = qrope.astype(BF16)

    def chunk(ch, carry):
        m, l, acc = carry
        slot = ch % 2
        nxt = ch + 1

        @pl.when(nxt < n_chunks)
        def _():
            start(b, nxt, 1 - slot)

        @pl.when(jnp.logical_and(nxt == n_chunks, b + 1 < nb))
        def _():
            start(b + 1, 0, 1 - slot)

        for cp in copies(b, ch, slot):
            cp.wait()
        kv = kvbuf[slot].reshape(pc * kvbuf.shape[2], KV_LORA).astype(BF16)
        kr = krbuf[slot].reshape(pc * krbuf.shape[2], MLA_ROPE).astype(BF16)
        s = _dot_nt(qlat_b, kv) + _dot_nt(qrope_b, kr)
        m_new = jnp.maximum(m, jnp.max(s, axis=-1, keepdims=True))
        alpha = jnp.exp(m - m_new)
        p = jnp.exp(s - m_new)
        l = alpha * l + jnp.sum(p, axis=-1, keepdims=True)
        acc = alpha * acc + _dot(p.astype(BF16), kv)
        return m_new, l, acc

    init = (jnp.full((MLA_HEADS, 1), NEG_BIG, F32), jnp.zeros((MLA_HEADS, 1), F32),
            jnp.zeros((MLA_HEADS, KV_LORA), F32))
    m, l, acc = lax.fori_loop(0, n_chunks, chunk, init)
    ckvn = ckvn_ref[...]
    s_new = (jnp.sum(qlat * ckvn, axis=-1, keepdims=True)
             + jnp.sum(qrope * krn_ref[...], axis=-1, keepdims=True))
    m_new = jnp.maximum(m, s_new)
    alpha = jnp.exp(m - m_new)
    p_new = jnp.exp(s_new - m_new)
    l = alpha * l + p_new
    lat_ref[...] = (alpha * acc + p_new * ckvn) / l


def _paged_attention(page_table, qlat, qrope, ckv_new, kr_new, cache_ckv, cache_kr):
    n, n_pages = page_table.shape
    page = cache_ckv.shape[1]
    pc = PAGES_PER_CHUNK
    per = lambda shape: pl.BlockSpec((None,) + shape, lambda bi, pt: (bi, 0, 0))
    grid_spec = pltpu.PrefetchScalarGridSpec(
        num_scalar_prefetch=1,
        grid=(n,),
        in_specs=[per((MLA_HEADS, KV_LORA)), per((MLA_HEADS, MLA_ROPE)), per((1, KV_LORA)), per((1, MLA_ROPE)),
                  pl.BlockSpec(memory_space=pl.ANY), pl.BlockSpec(memory_space=pl.ANY)],
        out_specs=per((MLA_HEADS, KV_LORA)),
        scratch_shapes=[pltpu.VMEM((2, pc, page, KV_LORA), F32), pltpu.VMEM((2, pc, page, MLA_ROPE), F32),
                        pltpu.SemaphoreType.DMA((2, 2))],
    )
    return pl.pallas_call(
        functools.partial(_paged_kernel, n_pages=n_pages),
        grid_spec=grid_spec,
        out_shape=jax.ShapeDtypeStruct((n, MLA_HEADS, KV_LORA), F32),
        compiler_params=_params("arbitrary"),
        name="paged_mla",
    )(page_table, qlat, qrope, ckv_new, kr_new, cache_ckv, cache_kr)


def _latv_kernel(lat_ref, wuv_ref, o_ref):
    outs = [_dot(lat_ref[hd].astype(BF16), wuv_ref[:, hd * MLA_VDIM:(hd + 1) * MLA_VDIM])
            for hd in range(MLA_HEADS)]
    o_ref[...] = jnp.concatenate(outs, axis=-1).astype(o_ref.dtype)


def _latv(lat_hb, wuv):
    n = lat_hb.shape[1]
    return pl.pallas_call(
        _latv_kernel,
        out_shape=jax.ShapeDtypeStruct((n, MLA_WIDTH), BF16),
        compiler_params=pltpu.CompilerParams(vmem_limit_bytes=VMEM_LIMIT),
        name="latv",
    )(lat_hb, wuv)


HG_SAMPLE_BLOCK = 8


def _hgrn_sample_kernel(s_ref, qcol_ref, fcol_ref, zh_ref, lbcol_ref, gn_ref, so_ref, o_ref):
    gn = gn_ref[...]
    rows = []
    for j in range(HG_SAMPLE_BLOCK):
        heads = []
        for hd in range(HG_HEADS):
            lb = _lower_bound(lbcol_ref[:, hd])
            lb = lb[0]
            f = lb + (1.0 - lb) * jax.nn.sigmoid(fcol_ref[j, hd])
            k = 1.0 - f
            v = zh_ref[j:j + 1, 2 * HG_QK + hd * HG_DV:2 * HG_QK + (hd + 1) * HG_DV]
            s_new = f * s_ref[j, hd] + k * v
            so_ref[j, hd] = s_new
            o = jnp.sum(qcol_ref[j, hd] * s_new, axis=0, keepdims=True)
            heads.append(_rms(o, gn))
        rows.append(jnp.concatenate(heads, axis=-1))
    o = jnp.concatenate(rows, axis=0)
    gate = zh_ref[:, 2 * HG_QK + HG_WIDTH:]
    o_ref[...] = (o * _silu(gate)).astype(o_ref.dtype)


def _hgrn_sample(state, qcol, fcol, zh, lbcol, gn):
    n = state.shape[0]
    nb = HG_SAMPLE_BLOCK
    blk4 = lambda last: pl.BlockSpec((nb, HG_HEADS, HG_DK, last), lambda i: (i, 0, 0, 0))
    return pl.pallas_call(
        _hgrn_sample_kernel,
        grid=(n // nb,),
        in_specs=[blk4(HG_DV), blk4(1), blk4(1), pl.BlockSpec((nb, zh.shape[1]), lambda i: (i, 0)),
                  _const_spec(lbcol.shape), _const_spec((1, HG_DV))],
        out_specs=[blk4(HG_DV), pl.BlockSpec((nb, HG_WIDTH), lambda i: (i, 0))],
        out_shape=[jax.ShapeDtypeStruct(state.shape, F32), jax.ShapeDtypeStruct((n, HG_WIDTH), BF16)],
        compiler_params=_params("parallel"),
        name="hgrn_sample",
    )(state, qcol, fcol, zh, lbcol, gn)


def _outproj_kernel(x_ref, omla_ref, ohg_ref, wout_ref, g_ref, wmq_ref, x1_ref, qm_ref):
    x1 = (x_ref[...] + _dot(omla_ref[...], wout_ref[:MLA_WIDTH, :])
          + _dot(ohg_ref[...], wout_ref[MLA_WIDTH:, :]))
    x1_ref[...] = x1
    hm = _rms(x1, g_ref[...]).astype(BF16)
    qm_ref[...] = (_dot(hm, wmq_ref[...]) * (MEM_HDIM ** -0.5)).astype(qm_ref.dtype)


def _outproj(x, omla, ohg, w, tm, q_dtype):
    rows = x.shape[0]
    row = lambda wd: pl.BlockSpec((tm, wd), lambda i: (i, 0))
    return pl.pallas_call(
        _outproj_kernel,
        grid=(rows // tm,),
        in_specs=[row(D_MODEL), row(MLA_WIDTH), row(HG_WIDTH), _const_spec((MLA_WIDTH + HG_WIDTH, D_MODEL)),
                  _const_spec((1, D_MODEL)), _const_spec((D_MODEL, MEM_WIDTH))],
        out_specs=[row(D_MODEL), row(MEM_WIDTH)],
        out_shape=[jax.ShapeDtypeStruct((rows, D_MODEL), F32), jax.ShapeDtypeStruct((rows, MEM_WIDTH), q_dtype)],
        compiler_params=_params("parallel"),
        name="outproj",
    )(x, omla, ohg, w["w_out"], w["norm_memx"], w["w_mq"])


def _memattn_prompt_kernel(q_ref, k_ref, v_ref, o_ref):
    outs = []
    for hd in range(MEM_HEADS):
        sl = slice(hd * MEM_HDIM, (hd + 1) * MEM_HDIM)
        s = _dot_nt(q_ref[:, sl], k_ref[:, sl])
        p = jnp.exp(s - jnp.max(s, axis=-1, keepdims=True))
        o = _dot(p.astype(BF16), v_ref[:, sl])
        outs.append(o / jnp.sum(p, axis=-1, keepdims=True))
    o_ref[...] = jnp.concatenate(outs, axis=-1).astype(o_ref.dtype)


def _memattn_prompt(q, k, v, tm):
    b, t, _ = q.shape
    m = k.shape[1]
    return pl.pallas_call(
        _memattn_prompt_kernel,
        grid=(b, t // tm),
        in_specs=[pl.BlockSpec((None, tm, MEM_WIDTH), lambda bi, ti: (bi, ti, 0)),
                  pl.BlockSpec((None, m, MEM_WIDTH), lambda bi, ti: (bi, 0, 0)),
                  pl.BlockSpec((None, m, MEM_WIDTH), lambda bi, ti: (bi, 0, 0))],
        out_specs=pl.BlockSpec((None, tm, MEM_WIDTH), lambda bi, ti: (bi, ti, 0)),
        out_shape=jax.ShapeDtypeStruct((b, t, MEM_WIDTH), BF16),
        compiler_params=_params("parallel", "parallel"),
        name="memattn_prompt",
    )(q, k, v)


def _memattn_sample_kernel(q_ref, k_ref, v_ref, o_ref):
    hid = lax.broadcasted_iota(jnp.int32, (SUBLANE, MEM_WIDTH), 0)
    lid = lax.broadcasted_iota(jnp.int32, (SUBLANE, MEM_WIDTH), 1) // MEM_HDIM
    own = hid == lid
    qbd = jnp.where(own, jnp.broadcast_to(q_ref[...], (SUBLANE, MEM_WIDTH)), 0.0).astype(BF16)
    s = _dot_nt(qbd, k_ref[...].astype(BF16))
    p = jnp.exp(s - jnp.max(s, axis=-1, keepdims=True))
    p = p / jnp.sum(p, axis=-1, keepdims=True)
    of = _dot(p.astype(BF16), v_ref[...].astype(BF16))
    o_ref[...] = jnp.sum(jnp.where(own, of, 0.0), axis=0, keepdims=True).astype(o_ref.dtype)


def _memattn_sample(q, k, v):
    n, m, _ = k.shape
    return pl.pallas_call(
        _memattn_sample_kernel,
        grid=(n,),
        in_specs=[pl.BlockSpec((None, 1, MEM_WIDTH), lambda i: (i, 0, 0)),
                  pl.BlockSpec((None, m, MEM_WIDTH), lambda i: (i, 0, 0)),
                  pl.BlockSpec((None, m, MEM_WIDTH), lambda i: (i, 0, 0))],
        out_specs=pl.BlockSpec((None, 1, MEM_WIDTH), lambda i: (i, 0, 0)),
        out_shape=jax.ShapeDtypeStruct((n, 1, MEM_WIDTH), BF16),
        compiler_params=_params("parallel"),
        name="memattn_sample",
    )(q, k, v)


def _ffn_kernel(*refs, seq_mode, tiles_per_seq):
    if seq_mode:
        (x1_ref, om_ref, wmo_ref, g_ref, wup_ref, cw_ref, cb_ref, wdn_ref, gfin_ref,
         y_ref, tail_ref, carry_ref) = refs
    else:
        (x1_ref, om_ref, wmo_ref, g_ref, wup_ref, cw_ref, cb_ref, wdn_ref, gfin_ref, prev2_ref, prev1_ref,
         y_ref, tail_ref) = refs
    tm = x1_ref.shape[0]
    x2 = x1_ref[...] + _dot(om_ref[...], wmo_ref[...])
    h = _rms(x2, g_ref[...]).astype(BF16)
    acc = x2
    if seq_mode:
        first = pl.program_id(0) % tiles_per_seq == 0
        rid = lax.broadcasted_iota(jnp.int32, (tm, 1), 0)

        @pl.when(pl.program_id(0) == 0)
        def _():
            carry_ref[...] = jnp.zeros_like(carry_ref)
    for f0 in range(0, D_FF, FFN_CHUNK):
        fs = slice(f0, f0 + FFN_CHUNK)
        a = _dot(h, wup_ref[:, fs])
        gt = _dot(h, wup_ref[:, D_FF + f0:D_FF + f0 + FFN_CHUNK])
        if seq_mode:
            prev = jnp.where(first, 0.0, carry_ref[:, fs])
            p1 = prev[SUBLANE - 1:SUBLANE]
            p2 = prev[SUBLANE - 2:SUBLANE - 1]
            a1 = jnp.where(rid == 0, p1, pltpu.roll(a, 1, axis=0))
            a2 = jnp.where(rid == 0, p2, jnp.where(rid == 1, p1, pltpu.roll(a, 2, axis=0)))
            carry_ref[:, fs] = a[tm - SUBLANE:, :]
            tail_ref[:, fs] = a[tm - (CONV_W - 1):, :]
        else:
            a1 = prev1_ref[:, fs]
            a2 = prev2_ref[:, fs]
            tail_ref[:, fs] = a
        conv = cb_ref[:, fs] + a2 * cw_ref[0:1, fs] + a1 * cw_ref[1:2, fs] + a * cw_ref[2:3, fs]
        u = (_silu(conv) * gt).astype(BF16)
        acc = acc + _dot(u, wdn_ref[fs, :])
    y_ref[...] = _rms(acc, gfin_ref[...])


def _ffn(x1, om, w, tm, seq_len=None, prev=None):
    rows = x1.shape[0]
    seq_mode = prev is None
    row = lambda wd: pl.BlockSpec((tm, wd), lambda i: (i, 0))
    in_specs = [row(D_MODEL), row(MEM_WIDTH), _const_spec((MEM_WIDTH, D_MODEL)), _const_spec((1, D_MODEL)),
                _const_spec((D_MODEL, 2 * D_FF)), _const_spec((CONV_W, D_FF)), _const_spec((1, D_FF)),
                _const_spec((D_FF, D_MODEL)), _const_spec((1, D_MODEL))]
    args = [x1, om, w["w_mo"], w["norm_ffn"], w["w_up"], w["conv_w"], w["conv_b"], w["w_down"], w["norm_final"]]
    if seq_mode:
        tiles_per_seq = seq_len // tm
        n_seq = rows // seq_len
        tail_spec = pl.BlockSpec((None, CONV_W - 1, D_FF), lambda i: (i // tiles_per_seq, 0, 0))
        tail_shape = jax.ShapeDtypeStruct((n_seq, CONV_W - 1, D_FF), F32)
        scratch = [pltpu.VMEM((SUBLANE, D_FF), F32)]
        sem = "arbitrary"
    else:
        tiles_per_seq = 1
        in_specs += [row(D_FF), row(D_FF)]
        args += list(prev)
        tail_spec = row(D_FF)
        tail_shape = jax.ShapeDtypeStruct((rows, D_FF), F32)
        scratch = []
        sem = "parallel"
    return pl.pallas_call(
        functools.partial(_ffn_kernel, seq_mode=seq_mode, tiles_per_seq=tiles_per_seq),
        grid=(rows // tm,),
        in_specs=in_specs,
        out_specs=[row(D_MODEL), tail_spec],
        out_shape=[jax.ShapeDtypeStruct((rows, D_MODEL), F32), tail_shape],
        scratch_shapes=scratch,
        compiler_params=_params(sem),
        name="ffn_seq" if seq_mode else "ffn_step",
    )(*args)


def _rope_tables(pos):
    half = ROPE_HALF
    inv = ROPE_BASE ** (-jnp.arange(half, dtype=F32) / half)
    ang = pos[:, None] * inv[None, :]
    cos, sin = jnp.cos(ang), jnp.sin(ang)
    n = pos.shape[0]
    one = jnp.ones((n, MLA_NOPE), F32)
    zero = jnp.zeros((n, MLA_NOPE), F32)
    pad = jnp.zeros((n, LANE - MLA_NOPE - MLA_ROPE), F32)
    cos_t = jnp.concatenate([one, cos, cos, pad], axis=-1)
    sin_t = jnp.concatenate([zero, -sin, sin, pad], axis=-1)
    return cos_t, sin_t


def _rope_tile(w_rope):
    x1, x2 = w_rope[..., :ROPE_HALF], w_rope[..., ROPE_HALF:]
    return jnp.concatenate([x1, x2, x2, x1], axis=-1)


def _prep_weights(norm_mix, w_in, norm_q, norm_kv, w_uq, w_uk, w_uv, norm_hg, w_out, norm_memx, norm_mem,
                  w_mq, w_mk, w_mv, w_mo, norm_ffn, w_up, conv_w, conv_b, w_down, norm_final):
    off_kr = Q_LORA + KV_LORA
    off_hg = off_kr + MLA_ROPE
    kr_tile = jnp.concatenate([jnp.zeros((D_MODEL, MLA_NOPE), F32), _rope_tile(w_in[:, off_kr:off_hg])], axis=-1)
    w_in_pad = jnp.concatenate([w_in[:, :off_kr], kr_tile, w_in[:, off_hg:]], axis=-1)
    uq = jnp.concatenate([w_uq[..., :MLA_NOPE], _rope_tile(w_uq[..., MLA_NOPE:])], axis=-1)
    uk = jnp.concatenate([w_uk, jnp.zeros((KV_LORA, MLA_HEADS, HEAD_PAD - MLA_NOPE), F32)], axis=-1)
    ukt = jnp.transpose(uk, (1, 2, 0))
    row = lambda g: g.reshape(1, -1)
    return {
        "norm_mix": row(norm_mix), "w_in": w_in_pad.astype(BF16), "norm_q": row(norm_q), "norm_kv": row(norm_kv),
        "w_uq": uq.reshape(Q_LORA, MLA_HEADS * HEAD_PAD).astype(BF16),
        "w_uk": uk.reshape(KV_LORA, MLA_HEADS * HEAD_PAD).astype(BF16),
        "w_ukt": ukt.astype(BF16),
        "w_uv": w_uv.reshape(KV_LORA, MLA_WIDTH).astype(BF16),
        "norm_hg": row(norm_hg), "w_out": w_out.astype(BF16), "norm_memx": row(norm_memx),
        "norm_mem": row(norm_mem), "w_mq": w_mq.reshape(D_MODEL, MEM_WIDTH).astype(BF16),
        "w_mk": w_mk.reshape(D_MODEL, MEM_WIDTH).astype(BF16), "w_mv": w_mv.reshape(D_MODEL, MEM_WIDTH).astype(BF16),
        "w_mo": w_mo.reshape(MEM_WIDTH, D_MODEL).astype(BF16), "norm_ffn": row(norm_ffn),
        "w_up": w_up.astype(BF16), "conv_w": conv_w, "conv_b": row(conv_b), "w_down": w_down.astype(BF16),
        "norm_final": row(norm_final),
    }


def kernel(x_prompt, x_sample, mem_prompt, cache_ckv, cache_krope, page_table, cache_mem_k, cache_mem_v,
           state_hgrn, state_conv, norm_mix, w_in, norm_q, norm_kv, w_uq, w_uk, w_uv, hg_lb_raw, norm_hg,
           w_out, norm_memx, norm_mem, w_mq, w_mk, w_mv, w_mo, norm_ffn, w_up, conv_w, conv_b, w_down,
           norm_final):
    bp, t, _ = x_prompt.shape
    db = x_sample.shape[0]
    n_pages = page_table.shape[1]
    page = cache_ckv.shape[2]
    past_len = n_pages * page
    w = _prep_weights(norm_mix[0], w_in[0], norm_q[0], norm_kv[0], w_uq[0], w_uk[0], w_uv[0], norm_hg[0],
                      w_out[0], norm_memx[0], norm_mem[0], w_mq[0], w_mk[0], w_mv[0], w_mo[0], norm_ffn[0],
                      w_up[0], conv_w[0], conv_b[0], w_down[0], norm_final)
    tm = ROW_TILE
    rows_p = bp * t

    mk_p, mv_p, mk_pb, mv_pb = _memkv(mem_prompt.reshape(-1, D_MODEL), w["norm_mem"], w["w_mk"], w["w_mv"])
    cos_p, sin_p = _rope_tables(jnp.arange(t, dtype=F32))
    xp = x_prompt.reshape(rows_p, D_MODEL)
    q_p, k_p, v_p, ckv_p, kr_p, zh_p = _mix_in(xp, w, cos_p, sin_p, tm)
    hq = MLA_HEADS * HEAD_PAD
    o_mla_p = _mla_prompt(q_p.reshape(bp, t, hq), k_p.reshape(bp, t, hq), v_p.reshape(bp, t, MLA_WIDTH), tm)
    o_hg_p, s_p = _hgrn_prompt(zh_p.reshape(bp, t, -1), hg_lb_raw, w["norm_hg"])
    x1_p, qm_p = _outproj(xp, o_mla_p.reshape(rows_p, -1), o_hg_p.reshape(rows_p, -1), w, tm, BF16)
    m_tok = mem_prompt.shape[1]
    om_p = _memattn_prompt(qm_p.reshape(bp, t, MEM_WIDTH), mk_pb.reshape(bp, m_tok, MEM_WIDTH),
                           mv_pb.reshape(bp, m_tok, MEM_WIDTH), tm)
    y_p, cv_p = _ffn(x1_p, om_p.reshape(rows_p, MEM_WIDTH), w, tm, seq_len=t)

    cos_s, sin_s = _rope_tables(jnp.full((db,), past_len, F32))
    xs = x_sample.reshape(db, D_MODEL)
    q_s, _, _, ckv_s, kr_s, zh_s = _mix_in(xs, w, cos_s, sin_s, db)
    qlat = jnp.transpose(_qlat(q_s, w["w_ukt"]), (1, 0, 2))
    qrope = q_s.reshape(db, MLA_HEADS, HEAD_PAD)[:, :, MLA_NOPE:MLA_NOPE + MLA_ROPE].astype(F32)
    kr_new = kr_s[:, MLA_NOPE:MLA_NOPE + MLA_ROPE]
    lat = _paged_attention(page_table, qlat, qrope, ckv_s.reshape(db, 1, KV_LORA), kr_new.reshape(db, 1, MLA_ROPE),
                           cache_ckv.reshape(-1, page, KV_LORA), cache_krope.reshape(-1, page, MLA_ROPE))
    o_mla_s = _latv(jnp.transpose(lat, (1, 0, 2)), w["w_uv"])
    qcol = zh_s[:, :HG_QK].reshape(db, HG_HEADS, HG_DK, 1)
    fcol = zh_s[:, HG_QK:2 * HG_QK].reshape(db, HG_HEADS, HG_DK, 1)
    lbcol = hg_lb_raw.reshape(-1, HG_HEADS, HG_DK, 1)
    s_s, o_hg_s = _hgrn_sample(state_hgrn.reshape(db, HG_HEADS, HG_DK, HG_DV), qcol, fcol, zh_s, lbcol, w["norm_hg"])
    x1_s, qm_s = _outproj(xs, o_mla_s, o_hg_s, w, db, F32)
    om_s = _memattn_sample(qm_s.reshape(db, 1, MEM_WIDTH), cache_mem_k.reshape(db, m_tok, MEM_WIDTH),
                           cache_mem_v.reshape(db, m_tok, MEM_WIDTH))
    y_s, a_s = _ffn(x1_s, om_s.reshape(db, MEM_WIDTH), w, db, prev=(state_conv[0, :, 0], state_conv[0, :, 1]))
    cv_s = jnp.stack([state_conv[0, :, 1], a_s], axis=1)

    rope_sl = slice(MLA_NOPE, MLA_NOPE + MLA_ROPE)
    return (y_p.reshape(bp, t, D_MODEL), y_s.reshape(db, 1, D_MODEL),
            ckv_p.reshape(1, bp, t, KV_LORA), kr_p[:, rope_sl].reshape(1, bp, t, MLA_ROPE),
            mk_p.reshape(1, bp, m_tok, MEM_HEADS, MEM_HDIM), mv_p.reshape(1, bp, m_tok, MEM_HEADS, MEM_HDIM),
            s_p[None], cv_p[None],
            ckv_s.reshape(1, db, 1, KV_LORA), kr_new.reshape(1, db, 1, MLA_ROPE),
            s_s[None], cv_s[None])
```

```python
import functools

import jax
import jax.numpy as jnp
from jax import lax
from jax.experimental import pallas as pl
from jax.experimental.pallas import tpu as pltpu

F32 = jnp.float32
BF16 = jnp.bfloat16

D_MODEL = 1024
MLA_HEADS = 8
MLA_NOPE = 64
MLA_ROPE = 32
MLA_VDIM = 64
Q_LORA = 384
KV_LORA = 256
HG_HEADS = 4
HG_DK = 128
HG_DV = 128
HG_CHUNK = 32
HG_QK = HG_HEADS * HG_DK
HG_WIDTH = HG_HEADS * HG_DV
MLA_WIDTH = MLA_HEADS * MLA_VDIM
MEM_HEADS = 4
MEM_HDIM = 128
MEM_WIDTH = MEM_HEADS * MEM_HDIM
D_FF = 2816
CONV_W = 3
ROPE_BASE = 10000.0
EPS = 1e-6

LANE = 128
SUBLANE = 8
HEAD_PAD = LANE
ROPE_HALF = MLA_ROPE // 2
OFF_CKV = Q_LORA
OFF_KRT = OFF_CKV + KV_LORA
OFF_HGP = OFF_KRT + LANE
IN_COLS_PAD = OFF_HGP + 2 * HG_QK + 2 * HG_WIDTH
VMEM_LIMIT = 56 * 1024 * 1024
ROW_TILE = 512
FFN_CHUNK = D_FF // 2
PAGES_PER_CHUNK = 16
PAGES_PER_BLOCK = 8
PAGED_SLOTS = 4
HG_GROUP = 256
NEG_BIG = -1e30


def _rms(x, g):
    return x * lax.rsqrt(jnp.mean(x * x, axis=-1, keepdims=True) + EPS) * g


def _dot(a, b):
    return jnp.dot(a, b, preferred_element_type=F32)


def _dot_nt(a, b):
    return lax.dot_general(a, b, (((1,), (1,)), ((), ())), preferred_element_type=F32)


def _dot_tn(a, b):
    return lax.dot_general(a, b, (((0,), (0,)), ((), ())), preferred_element_type=F32)


def _silu(x):
    return x * jax.nn.sigmoid(x)


def _params(*sem):
    return pltpu.CompilerParams(dimension_semantics=sem, vmem_limit_bytes=VMEM_LIMIT)


def _const_spec(shape):
    nd = len(shape)
    return pl.BlockSpec(shape, lambda *_: (0,) * nd, pipeline_mode=pl.Buffered(1))


def _memkv_kernel(mem_ref, g_ref, wk_ref, wv_ref, k_ref, v_ref, kb_ref, vb_ref):
    m = _rms(mem_ref[...], g_ref[...]).astype(BF16)
    k = _dot(m, wk_ref[...])
    v = _dot(m, wv_ref[...])
    k_ref[...] = k
    v_ref[...] = v
    kb_ref[...] = k.astype(BF16)
    vb_ref[...] = v.astype(BF16)


def _memkv(mem, g, wk, wv):
    rows = mem.shape[0]
    tm = min(ROW_TILE, rows)
    row = lambda w: pl.BlockSpec((tm, w), lambda i: (i, 0))
    return pl.pallas_call(
        _memkv_kernel,
        grid=(rows // tm,),
        in_specs=[row(D_MODEL), _const_spec((1, D_MODEL)), _const_spec((D_MODEL, MEM_WIDTH)),
                  _const_spec((D_MODEL, MEM_WIDTH))],
        out_specs=[row(MEM_WIDTH)] * 4,
        out_shape=[jax.ShapeDtypeStruct((rows, MEM_WIDTH), F32)] * 2
        + [jax.ShapeDtypeStruct((rows, MEM_WIDTH), BF16)] * 2,
        compiler_params=_params("parallel"),
        name="memkv",
    )(mem, g, wk, wv)


def _mix_in_kernel(x_ref, g_ref, win_ref, gq_ref, gkv_ref, wuq_ref, wuk_ref, wuv_ref, cos_ref, sin_ref,
                   q_ref, k_ref, v_ref, ckv_ref, kr_ref, zh_ref):
    h = _rms(x_ref[...], g_ref[...]).astype(BF16)
    z = _dot(h, win_ref[...])
    c_q = _rms(z[:, :OFF_CKV], gq_ref[...]).astype(BF16)
    c_kv = _rms(z[:, OFF_CKV:OFF_KRT], gkv_ref[...])
    ckv_ref[...] = c_kv
    c_kvb = c_kv.astype(BF16)
    zh_ref[...] = z[:, OFF_HGP:]
    cos = cos_ref[...]
    sin = sin_ref[...]
    zkr = z[:, OFF_KRT:OFF_HGP]
    kr = zkr * cos + pltpu.roll(zkr, LANE - MLA_ROPE, axis=1) * sin
    kr_ref[...] = kr
    scale = (MLA_NOPE + MLA_ROPE) ** -0.5
    cos_q = cos * scale
    sin_q = sin * scale
    q = _dot(c_q, wuq_ref[...])
    kn = _dot(c_kvb, wuk_ref[...])
    for hd in range(MLA_HEADS):
        sl = slice(hd * HEAD_PAD, (hd + 1) * HEAD_PAD)
        qh = q[:, sl]
        q_ref[:, sl] = (qh * cos_q + pltpu.roll(qh, LANE - MLA_ROPE, axis=1) * sin_q).astype(BF16)
        k_ref[:, sl] = (kn[:, sl] + kr).astype(BF16)
    v_ref[...] = _dot(c_kvb, wuv_ref[...]).astype(BF16)


def _mix_in(x, w, cos_t, sin_t, tm):
    rows = x.shape[0]
    t_tiles = cos_t.shape[0] // tm
    row = lambda wd: pl.BlockSpec((tm, wd), lambda i: (i, 0))
    tab = pl.BlockSpec((tm, LANE), lambda i: (i % t_tiles, 0))
    hq = MLA_HEADS * HEAD_PAD
    return pl.pallas_call(
        _mix_in_kernel,
        grid=(rows // tm,),
        in_specs=[row(D_MODEL), _const_spec((1, D_MODEL)), _const_spec((D_MODEL, IN_COLS_PAD)),
                  _const_spec((1, Q_LORA)), _const_spec((1, KV_LORA)), _const_spec((Q_LORA, hq)),
                  _const_spec((KV_LORA, hq)), _const_spec((KV_LORA, MLA_WIDTH)), tab, tab],
        out_specs=[row(hq), row(hq), row(MLA_WIDTH), row(KV_LORA), row(LANE), row(4 * HG_QK)],
        out_shape=[jax.ShapeDtypeStruct((rows, hq), BF16), jax.ShapeDtypeStruct((rows, hq), BF16),
                   jax.ShapeDtypeStruct((rows, MLA_WIDTH), BF16), jax.ShapeDtypeStruct((rows, KV_LORA), F32),
                   jax.ShapeDtypeStruct((rows, LANE), F32), jax.ShapeDtypeStruct((rows, 4 * HG_QK), F32)],
        compiler_params=_params("parallel"),
        name="mix_in",
    )(x, w["norm_mix"], w["w_in"], w["norm_q"], w["norm_kv"], w["w_uq"], w["w_uk"], w["w_uv"], cos_t, sin_t)


HEADS_PER_STEP = 2


def _mla_prompt_kernel(q_ref, k_ref, v_ref, o_ref, *, bq):
    qi = pl.program_id(2)
    outs = []
    for hh in range(HEADS_PER_STEP):
        ksl = slice(hh * HEAD_PAD, (hh + 1) * HEAD_PAD)
        vsl = slice(hh * MLA_VDIM, (hh + 1) * MLA_VDIM)
        q = q_ref[:, ksl]

        def block(j, carry, masked):
            m, l, acc = carry
            rows = pl.ds(pl.multiple_of(j * bq, bq), bq)
            s = _dot_nt(q, k_ref[rows, ksl])
            if masked:
                r = lax.broadcasted_iota(jnp.int32, (bq, bq), 0)
                c = lax.broadcasted_iota(jnp.int32, (bq, bq), 1)
                s = jnp.where(c <= r, s, NEG_BIG)
            m_new = jnp.maximum(m, jnp.max(s, axis=-1, keepdims=True))
            alpha = jnp.exp(m - m_new)
            p = jnp.exp(s - m_new)
            l = alpha * l + jnp.sum(p, axis=-1, keepdims=True)
            acc = alpha * acc + _dot(p.astype(BF16), v_ref[rows, vsl])
            return m_new, l, acc

        init = (jnp.full((bq, 1), NEG_BIG, F32), jnp.zeros((bq, 1), F32), jnp.zeros((bq, MLA_VDIM), F32))
        carry = lax.fori_loop(0, qi, functools.partial(block, masked=False), init)
        m, l, acc = block(qi, carry, True)
        outs.append(acc / l)
    o_ref[...] = jnp.concatenate(outs, axis=-1).astype(o_ref.dtype)


def _mla_prompt(q, k, v, bq):
    b, t, _ = q.shape
    hp = HEADS_PER_STEP
    return pl.pallas_call(
        functools.partial(_mla_prompt_kernel, bq=bq),
        grid=(b, MLA_HEADS // hp, t // bq),
        in_specs=[pl.BlockSpec((None, bq, hp * HEAD_PAD), lambda bi, hi, qi: (bi, qi, hi)),
                  pl.BlockSpec((None, t, hp * HEAD_PAD), lambda bi, hi, qi: (bi, 0, hi)),
                  pl.BlockSpec((None, t, hp * MLA_VDIM), lambda bi, hi, qi: (bi, 0, hi))],
        out_specs=pl.BlockSpec((None, bq, hp * MLA_VDIM), lambda bi, hi, qi: (bi, qi, hi)),
        out_shape=jax.ShapeDtypeStruct((b, t, MLA_WIDTH), BF16),
        compiler_params=_params("parallel", "parallel", "arbitrary"),
        name="mla_prompt",
    )(q, k, v)


def _lower_bound(raw):
    e = jnp.exp(raw - jnp.max(raw, axis=0, keepdims=True))
    return e[0:1] / jnp.sum(e, axis=0, keepdims=True)


def _hgrn_prompt_kernel(q_ref, f_ref, i_ref, gate_ref, lbraw_ref, gn_ref, o_ref, s_ref,
                        qt_scr, o_scr, b_scr, u_scr, *, t):
    c = HG_CHUNK
    g = HG_GROUP
    cpg = g // c
    lb = _lower_bound(lbraw_ref[...])
    pos = lax.broadcasted_iota(jnp.int32, (g, HG_DK), 0) % c
    r2 = lax.broadcasted_iota(jnp.int32, (g, g), 0)
    c2 = lax.broadcasted_iota(jnp.int32, (g, g), 1)
    amask = jnp.logical_and(c2 <= r2, c2 >= r2 - r2 % c)

    def group(gi, carry):
        rows = pl.ds(pl.multiple_of(gi * g, g), g)
        q = q_ref[rows, :]
        vb = i_ref[rows, :].astype(BF16)
        f = lb + (1.0 - lb) * jax.nn.sigmoid(f_ref[rows, :])
        k = 1.0 - f
        bcum = jnp.log(f)
        for sh in (1, 2, 4, 8, 16):
            bcum = bcum + jnp.where(pos >= sh, pltpu.roll(bcum, sh, axis=0), 0.0)
        b_scr[rows, :] = bcum
        q_t = (q * jnp.exp(bcum)).astype(BF16)
        k_t = (k * jnp.exp(-bcum)).astype(BF16)
        qt_scr[rows, :] = q_t
        a = jnp.where(amask, _dot_nt(q_t, k_t), 0.0)
        o_scr[rows, :] = _dot(a.astype(BF16), vb)
        b3 = bcum.reshape(cpg, c, HG_DK)
        k_s = (k.reshape(cpg, c, HG_DK) * jnp.exp(b3[:, c - 1:c, :] - b3)).astype(BF16)
        v3 = vb.reshape(cpg, c, HG_DV)
        for ci in range(cpg):
            u_scr[gi * cpg + ci] = _dot_tn(v3[ci], k_s[ci])
        return carry

    lax.fori_loop(0, t // g, group, 0)

    def step(ci, st):
        rows = pl.ds(pl.multiple_of(ci * c, c), c)
        o_scr[rows, :] = o_scr[rows, :] + _dot_nt(qt_scr[rows, :], st.astype(BF16))
        decay = jnp.exp(b_scr[pl.ds(ci * c + c - 1, 1), :])
        return st * decay + u_scr[ci]

    st = lax.fori_loop(0, t // c, step, jnp.zeros((HG_DV, HG_DK), F32), unroll=4)
    s_ref[...] = st.T

    gn = gn_ref[...]

    def finish(gi, carry):
        rows = pl.ds(pl.multiple_of(gi * g, g), g)
        o_ref[rows, :] = (_rms(o_scr[rows, :], gn) * _silu(gate_ref[rows, :])).astype(o_ref.dtype)
        return carry

    lax.fori_loop(0, t // g, finish, 0)


def _hgrn_prompt(zh, lb_raw, gn):
    b, t, _ = zh.shape
    col = lambda off: pl.BlockSpec((None, t, HG_DK), lambda bi, hi: (bi, 0, off + hi))
    return pl.pallas_call(
        functools.partial(_hgrn_prompt_kernel, t=t),
        grid=(b, HG_HEADS),
        in_specs=[col(0), col(HG_HEADS), col(2 * HG_HEADS), col(3 * HG_HEADS),
                  pl.BlockSpec((lb_raw.shape[0], HG_DK), lambda bi, hi: (0, hi)), _const_spec((1, HG_DV))],
        out_specs=[pl.BlockSpec((None, t, HG_DV), lambda bi, hi: (bi, 0, hi)),
                   pl.BlockSpec((None, None, HG_DK, HG_DV), lambda bi, hi: (bi, hi, 0, 0))],
        out_shape=[jax.ShapeDtypeStruct((b, t, HG_WIDTH), BF16),
                   jax.ShapeDtypeStruct((b, HG_HEADS, HG_DK, HG_DV), F32)],
        scratch_shapes=[pltpu.VMEM((t, HG_DK), BF16), pltpu.VMEM((t, HG_DV), F32), pltpu.VMEM((t, HG_DK), F32),
                        pltpu.VMEM((t // HG_CHUNK, HG_DV, HG_DK), F32)],
        compiler_params=_params("parallel", "parallel"),
        name="hgrn_prompt",
    )(zh, zh, zh, zh, lb_raw, gn)


def _qlat_kernel(q_ref, wukt_ref, o_ref):
    for hd in range(MLA_HEADS):
        o_ref[hd] = _dot(q_ref[:, hd * HEAD_PAD:(hd + 1) * HEAD_PAD], wukt_ref[hd])


def _qlat(q_pad, wukt):
    n = q_pad.shape[0]
    return pl.pallas_call(
        _qlat_kernel,
        out_shape=jax.ShapeDtypeStruct((MLA_HEADS, n, KV_LORA), F32),
        compiler_params=pltpu.CompilerParams(vmem_limit_bytes=VMEM_LIMIT),
        name="qlat",
    )(q_pad, wukt)


def _paged_kernel(pt_ref, qlat_ref, qrope_ref, ckvn_ref, krn_ref, ckv_hbm, krt_hbm, lat_ref,
                  kvbuf, krbuf, sem, *, n_pages):
    b = pl.program_id(0)
    nb = pl.num_programs(0)
    pc = PAGES_PER_CHUNK
    pb = PAGES_PER_BLOCK
    ns = PAGED_SLOTS
    ahead = ns - 1
    n_chunks = n_pages // pc
    page = kvbuf.shape[2]

    def copies(bb, ch, slot):
        out = []
        for p in range(pc):
            pg = pt_ref[bb, ch * pc + p]
            out.append(pltpu.make_async_copy(ckv_hbm.at[pg], kvbuf.at[slot, p], sem.at[0, slot]))
            out.append(pltpu.make_async_copy(krt_hbm.at[pg], krbuf.at[slot, p], sem.at[1, slot]))
        return out

    def start(bb, ch, slot):
        for cp in copies(bb, ch, slot):
            cp.start()

    @pl.when(b == 0)
    def _():
        for ch0 in range(ahead):
            start(0, ch0, ch0)

    qlat = qlat_ref[...]
    qrope = qrope_ref[...]
    qlat_b = qlat.astype(BF16)
    qrope_b = qrope.astype(BF16)

    def chunk(ch, carry):
        m, l, acc = carry
        tgt = ch + ahead
        tslot = tgt % ns

        @pl.when(tgt < n_chunks)
        def _():
            start(b, tgt, tslot)

        @pl.when(jnp.logical_and(tgt >= n_chunks, b + 1 < nb))
        def _():
            start(b + 1, tgt - n_chunks, tslot)

        slot = ch % ns
        for cp in copies(b, ch, slot):
            cp.wait()
        blocks = []
        for j in range(pc // pb):
            kv = kvbuf[slot, pl.ds(j * pb, pb)].reshape(pb * page, KV_LORA).astype(BF16)
            krt = jnp.concatenate([krbuf[slot, j * pb + p] for p in range(pb)], axis=1).astype(BF16)
            blocks.append((kv, _dot_nt(qlat_b, kv) + _dot(qrope_b, krt)))
        m_new = m
        for _, s in blocks:
            m_new = jnp.maximum(m_new, jnp.max(s, axis=-1, keepdims=True))
        alpha = jnp.exp(m - m_new)
        l = alpha * l
        acc = alpha * acc
        for kv, s in blocks:
            p = jnp.exp(s - m_new)
            l = l + jnp.sum(p, axis=-1, keepdims=True)
            acc = acc + _dot(p.astype(BF16), kv)
        return m_new, l, acc

    init = (jnp.full((MLA_HEADS, 1), NEG_BIG, F32), jnp.zeros((MLA_HEADS, 1), F32),
            jnp.zeros((MLA_HEADS, KV_LORA), F32))
    m, l, acc = lax.fori_loop(0, n_chunks, chunk, init)
    ckvn = ckvn_ref[...]
    s_new = (jnp.sum(qlat * ckvn, axis=-1, keepdims=True)
             + jnp.sum(qrope * krn_ref[...], axis=-1, keepdims=True))
    m_new = jnp.maximum(m, s_new)
    alpha = jnp.exp(m - m_new)
    p_new = jnp.exp(s_new - m_new)
    l = alpha * l + p_new
    lat_ref[...] = (alpha * acc + p_new * ckvn) / l


def _paged_attention(page_table, qlat, qrope, ckv_new, kr_new, cache_ckv, cache_kr):
    n, n_pages = page_table.shape
    page = cache_ckv.shape[1]
    pc = PAGES_PER_CHUNK
    per = lambda shape: pl.BlockSpec((None,) + shape, lambda bi, pt: (bi, 0, 0))
    grid_spec = pltpu.PrefetchScalarGridSpec(
        num_scalar_prefetch=1,
        grid=(n,),
        in_specs=[per((MLA_HEADS, KV_LORA)), per((MLA_HEADS, MLA_ROPE)), per((1, KV_LORA)), per((1, MLA_ROPE)),
                  pl.BlockSpec(memory_space=pl.ANY), pl.BlockSpec(memory_space=pl.ANY)],
        out_specs=per((MLA_HEADS, KV_LORA)),
        scratch_shapes=[pltpu.VMEM((PAGED_SLOTS, pc, page, KV_LORA), F32),
                        pltpu.VMEM((PAGED_SLOTS, pc, MLA_ROPE, page), F32),
                        pltpu.SemaphoreType.DMA((2, PAGED_SLOTS))],
    )
    return pl.pallas_call(
        functools.partial(_paged_kernel, n_pages=n_pages),
        grid_spec=grid_spec,
        out_shape=jax.ShapeDtypeStruct((n, MLA_HEADS, KV_LORA), F32),
        compiler_params=_params("arbitrary"),
        name="paged_mla",
    )(page_table, qlat, qrope, ckv_new, kr_new, cache_ckv, cache_kr)


def _latv_kernel(lat_ref, wuv_ref, o_ref):
    outs = [_dot(lat_ref[hd].astype(BF16), wuv_ref[:, hd * MLA_VDIM:(hd + 1) * MLA_VDIM])
            for hd in range(MLA_HEADS)]
    o_ref[...] = jnp.concatenate(outs, axis=-1).astype(o_ref.dtype)


def _latv(lat_hb, wuv):
    n = lat_hb.shape[1]
    return pl.pallas_call(
        _latv_kernel,
        out_shape=jax.ShapeDtypeStruct((n, MLA_WIDTH), BF16),
        compiler_params=pltpu.CompilerParams(vmem_limit_bytes=VMEM_LIMIT),
        name="latv",
    )(lat_hb, wuv)


HG_SAMPLE_BLOCK = 8


def _hgrn_sample_kernel(s_ref, qcol_ref, fcol_ref, zh_ref, lbcol_ref, gn_ref, so_ref, o_ref):
    gn = gn_ref[...]
    rows = []
    for j in range(HG_SAMPLE_BLOCK):
        heads = []
        for hd in range(HG_HEADS):
            lb = _lower_bound(lbcol_ref[:, hd])
            lb = lb[0]
            f = lb + (1.0 - lb) * jax.nn.sigmoid(fcol_ref[j, hd])
            k = 1.0 - f
            v = zh_ref[j:j + 1, 2 * HG_QK + hd * HG_DV:2 * HG_QK + (hd + 1) * HG_DV]
            s_new = f * s_ref[j, hd] + k * v
            so_ref[j, hd] = s_new
            o = jnp.sum(qcol_ref[j, hd] * s_new, axis=0, keepdims=True)
            heads.append(_rms(o, gn))
        rows.append(jnp.concatenate(heads, axis=-1))
    o = jnp.concatenate(rows, axis=0)
    gate = zh_ref[:, 2 * HG_QK + HG_WIDTH:]
    o_ref[...] = (o * _silu(gate)).astype(o_ref.dtype)


def _hgrn_sample(state, qcol, fcol, zh, lbcol, gn):
    n = state.shape[0]
    nb = HG_SAMPLE_BLOCK
    blk4 = lambda last: pl.BlockSpec((nb, HG_HEADS, HG_DK, last), lambda i: (i, 0, 0, 0))
    return pl.pallas_call(
        _hgrn_sample_kernel,
        grid=(n // nb,),
        in_specs=[blk4(HG_DV), blk4(1), blk4(1), pl.BlockSpec((nb, zh.shape[1]), lambda i: (i, 0)),
                  _const_spec(lbcol.shape), _const_spec((1, HG_DV))],
        out_specs=[blk4(HG_DV), pl.BlockSpec((nb, HG_WIDTH), lambda i: (i, 0))],
        out_shape=[jax.ShapeDtypeStruct(state.shape, F32), jax.ShapeDtypeStruct((n, HG_WIDTH), BF16)],
        compiler_params=_params("parallel"),
        name="hgrn_sample",
    )(state, qcol, fcol, zh, lbcol, gn)


def _outproj_kernel(x_ref, omla_ref, ohg_ref, wout_ref, g_ref, wmq_ref, x1_ref, qm_ref):
    x1 = (x_ref[...] + _dot(omla_ref[...], wout_ref[:MLA_WIDTH, :])
          + _dot(ohg_ref[...], wout_ref[MLA_WIDTH:, :]))
    x1_ref[...] = x1
    hm = _rms(x1, g_ref[...]).astype(BF16)
    qm_ref[...] = (_dot(hm, wmq_ref[...]) * (MEM_HDIM ** -0.5)).astype(qm_ref.dtype)


def _outproj(x, omla, ohg, w, tm, q_dtype):
    rows = x.shape[0]
    row = lambda wd: pl.BlockSpec((tm, wd), lambda i: (i, 0))
    return pl.pallas_call(
        _outproj_kernel,
        grid=(rows // tm,),
        in_specs=[row(D_MODEL), row(MLA_WIDTH), row(HG_WIDTH), _const_spec((MLA_WIDTH + HG_WIDTH, D_MODEL)),
                  _const_spec((1, D_MODEL)), _const_spec((D_MODEL, MEM_WIDTH))],
        out_specs=[row(D_MODEL), row(MEM_WIDTH)],
        out_shape=[jax.ShapeDtypeStruct((rows, D_MODEL), F32), jax.ShapeDtypeStruct((rows, MEM_WIDTH), q_dtype)],
        compiler_params=_params("parallel"),
        name="outproj",
    )(x, omla, ohg, w["w_out"], w["norm_memx"], w["w_mq"])


def _memattn_prompt_kernel(q_ref, k_ref, v_ref, o_ref):
    outs = []
    for hd in range(MEM_HEADS):
        sl = slice(hd * MEM_HDIM, (hd + 1) * MEM_HDIM)
        s = _dot_nt(q_ref[:, sl], k_ref[:, sl])
        p = jnp.exp(s - jnp.max(s, axis=-1, keepdims=True))
        o = _dot(p.astype(BF16), v_ref[:, sl])
        outs.append(o / jnp.sum(p, axis=-1, keepdims=True))
    o_ref[...] = jnp.concatenate(outs, axis=-1).astype(o_ref.dtype)


def _memattn_prompt(q, k, v, tm):
    b, t, _ = q.shape
    m = k.shape[1]
    return pl.pallas_call(
        _memattn_prompt_kernel,
        grid=(b, t // tm),
        in_specs=[pl.BlockSpec((None, tm, MEM_WIDTH), lambda bi, ti: (bi, ti, 0)),
                  pl.BlockSpec((None, m, MEM_WIDTH), lambda bi, ti: (bi, 0, 0)),
                  pl.BlockSpec((None, m, MEM_WIDTH), lambda bi, ti: (bi, 0, 0))],
        out_specs=pl.BlockSpec((None, tm, MEM_WIDTH), lambda bi, ti: (bi, ti, 0)),
        out_shape=jax.ShapeDtypeStruct((b, t, MEM_WIDTH), BF16),
        compiler_params=_params("parallel", "parallel"),
        name="memattn_prompt",
    )(q, k, v)


def _memattn_sample_kernel(q_ref, k_ref, v_ref, o_ref):
    rows = k_ref.shape[0]
    q = jnp.concatenate([q_ref[:, hd * MEM_HDIM:(hd + 1) * MEM_HDIM] for hd in range(MEM_HEADS)]
                        + [jnp.zeros((SUBLANE - MEM_HEADS, MEM_HDIM), F32)], axis=0)
    s = _dot_nt(q.astype(BF16), k_ref[...].astype(BF16))
    hid = lax.broadcasted_iota(jnp.int32, (SUBLANE, rows), 0)
    rhd = lax.broadcasted_iota(jnp.int32, (SUBLANE, rows), 1) % MEM_HEADS
    s = jnp.where(hid == rhd, s, NEG_BIG)
    p = jnp.exp(s - jnp.max(s, axis=-1, keepdims=True))
    p = jnp.where(hid == rhd, p, 0.0)
    p = p / jnp.sum(p, axis=-1, keepdims=True)
    of = _dot(p.astype(BF16), v_ref[...].astype(BF16))
    o_ref[...] = jnp.concatenate([of[hd:hd + 1] for hd in range(MEM_HEADS)], axis=-1).astype(o_ref.dtype)


def _memattn_sample(q, k, v):
    n, m = k.shape[:2]
    return pl.pallas_call(
        _memattn_sample_kernel,
        grid=(n,),
        in_specs=[pl.BlockSpec((None, 1, MEM_WIDTH), lambda i: (i, 0, 0)),
                  pl.BlockSpec((None, m, MEM_HDIM), lambda i: (i, 0, 0)),
                  pl.BlockSpec((None, m, MEM_HDIM), lambda i: (i, 0, 0))],
        out_specs=pl.BlockSpec((None, 1, MEM_WIDTH), lambda i: (i, 0, 0)),
        out_shape=jax.ShapeDtypeStruct((n, 1, MEM_WIDTH), BF16),
        compiler_params=_params("parallel"),
        name="memattn_sample",
    )(q, k, v)


def _ffn_kernel(*refs, seq_mode, tiles_per_seq):
    if seq_mode:
        (x1_ref, om_ref, wmo_ref, g_ref, wup_ref, cw_ref, cb_ref, wdn_ref, gfin_ref,
         y_ref, tail_ref, carry_ref) = refs
    else:
        (x1_ref, om_ref, wmo_ref, g_ref, wup_ref, cw_ref, cb_ref, wdn_ref, gfin_ref, prev2_ref, prev1_ref,
         y_ref, tail_ref) = refs
    tm = x1_ref.shape[0]
    x2 = x1_ref[...] + _dot(om_ref[...], wmo_ref[...])
    h = _rms(x2, g_ref[...]).astype(BF16)
    acc = x2
    if seq_mode:
        first = pl.program_id(0) % tiles_per_seq == 0
        rid = lax.broadcasted_iota(jnp.int32, (tm, 1), 0)

        @pl.when(pl.program_id(0) == 0)
        def _():
            carry_ref[...] = jnp.zeros_like(carry_ref)
    for f0 in range(0, D_FF, FFN_CHUNK):
        fs = slice(f0, f0 + FFN_CHUNK)
        a = _dot(h, wup_ref[:, fs])
        gt = _dot(h, wup_ref[:, D_FF + f0:D_FF + f0 + FFN_CHUNK])
        if seq_mode:
            prev = jnp.where(first, 0.0, carry_ref[:, fs])
            p1 = prev[SUBLANE - 1:SUBLANE]
            p2 = prev[SUBLANE - 2:SUBLANE - 1]
            a1 = jnp.where(rid == 0, p1, pltpu.roll(a, 1, axis=0))
            a2 = jnp.where(rid == 0, p2, jnp.where(rid == 1, p1, pltpu.roll(a, 2, axis=0)))
            carry_ref[:, fs] = a[tm - SUBLANE:, :]
            tail_ref[:, fs] = a[tm - (CONV_W - 1):, :]
        else:
            a1 = prev1_ref[:, fs]
            a2 = prev2_ref[:, fs]
            tail_ref[:, fs] = a
        conv = cb_ref[:, fs] + a2 * cw_ref[0:1, fs] + a1 * cw_ref[1:2, fs] + a * cw_ref[2:3, fs]
        u = (_silu(conv) * gt).astype(BF16)
        acc = acc + _dot(u, wdn_ref[fs, :])
    y_ref[...] = _rms(acc, gfin_ref[...])


def _ffn(x1, om, w, tm, seq_len=None, prev=None):
    rows = x1.shape[0]
    seq_mode = prev is None
    row = lambda wd: pl.BlockSpec((tm, wd), lambda i: (i, 0))
    in_specs = [row(D_MODEL), row(MEM_WIDTH), _const_spec((MEM_WIDTH, D_MODEL)), _const_spec((1, D_MODEL)),
                _const_spec((D_MODEL, 2 * D_FF)), _const_spec((CONV_W, D_FF)), _const_spec((1, D_FF)),
                _const_spec((D_FF, D_MODEL)), _const_spec((1, D_MODEL))]
    args = [x1, om, w["w_mo"], w["norm_ffn"], w["w_up"], w["conv_w"], w["conv_b"], w["w_down"], w["norm_final"]]
    if seq_mode:
        tiles_per_seq = seq_len // tm
        n_seq = rows // seq_len
        tail_spec = pl.BlockSpec((None, CONV_W - 1, D_FF), lambda i: (i // tiles_per_seq, 0, 0))
        tail_shape = jax.ShapeDtypeStruct((n_seq, CONV_W - 1, D_FF), F32)
        scratch = [pltpu.VMEM((SUBLANE, D_FF), F32)]
        sem = "arbitrary"
    else:
        tiles_per_seq = 1
        in_specs += [row(D_FF), row(D_FF)]
        args += list(prev)
        tail_spec = row(D_FF)
        tail_shape = jax.ShapeDtypeStruct((rows, D_FF), F32)
        scratch = []
        sem = "parallel"
    return pl.pallas_call(
        functools.partial(_ffn_kernel, seq_mode=seq_mode, tiles_per_seq=tiles_per_seq),
        grid=(rows // tm,),
        in_specs=in_specs,
        out_specs=[row(D_MODEL), tail_spec],
        out_shape=[jax.ShapeDtypeStruct((rows, D_MODEL), F32), tail_shape],
        scratch_shapes=scratch,
        compiler_params=_params(sem),
        name="ffn_seq" if seq_mode else "ffn_step",
    )(*args)


def _rope_tables(pos):
    half = ROPE_HALF
    inv = ROPE_BASE ** (-jnp.arange(half, dtype=F32) / half)
    ang = pos[:, None] * inv[None, :]
    cos, sin = jnp.cos(ang), jnp.sin(ang)
    n = pos.shape[0]
    one = jnp.ones((n, MLA_NOPE), F32)
    zero = jnp.zeros((n, MLA_NOPE), F32)
    pad = jnp.zeros((n, LANE - MLA_NOPE - MLA_ROPE), F32)
    cos_t = jnp.concatenate([one, cos, cos, pad], axis=-1)
    sin_t = jnp.concatenate([zero, -sin, sin, pad], axis=-1)
    return cos_t, sin_t


def _rope_tile(w_rope):
    x1, x2 = w_rope[..., :ROPE_HALF], w_rope[..., ROPE_HALF:]
    return jnp.concatenate([x1, x2, x2, x1], axis=-1)


def _prep_weights(norm_mix, w_in, norm_q, norm_kv, w_uq, w_uk, w_uv, norm_hg, w_out, norm_memx, norm_mem,
                  w_mq, w_mk, w_mv, w_mo, norm_ffn, w_up, conv_w, conv_b, w_down, norm_final):
    off_kr = Q_LORA + KV_LORA
    off_hg = off_kr + MLA_ROPE
    kr_tile = jnp.concatenate([jnp.zeros((D_MODEL, MLA_NOPE), F32), _rope_tile(w_in[:, off_kr:off_hg])], axis=-1)
    w_in_pad = jnp.concatenate([w_in[:, :off_kr], kr_tile, w_in[:, off_hg:]], axis=-1)
    uq = jnp.concatenate([w_uq[..., :MLA_NOPE], _rope_tile(w_uq[..., MLA_NOPE:])], axis=-1)
    uk = jnp.concatenate([w_uk, jnp.zeros((KV_LORA, MLA_HEADS, HEAD_PAD - MLA_NOPE), F32)], axis=-1)
    ukt = jnp.transpose(uk, (1, 2, 0))
    row = lambda g: g.reshape(1, -1)
    return {
        "norm_mix": row(norm_mix), "w_in": w_in_pad.astype(BF16), "norm_q": row(norm_q), "norm_kv": row(norm_kv),
        "w_uq": uq.reshape(Q_LORA, MLA_HEADS * HEAD_PAD).astype(BF16),
        "w_uk": uk.reshape(KV_LORA, MLA_HEADS * HEAD_PAD).astype(BF16),
        "w_ukt": ukt.astype(BF16),
        "w_uv": w_uv.reshape(KV_LORA, MLA_WIDTH).astype(BF16),
        "norm_hg": row(norm_hg), "w_out": w_out.astype(BF16), "norm_memx": row(norm_memx),
        "norm_mem": row(norm_mem), "w_mq": w_mq.reshape(D_MODEL, MEM_WIDTH).astype(BF16),
        "w_mk": w_mk.reshape(D_MODEL, MEM_WIDTH).astype(BF16), "w_mv": w_mv.reshape(D_MODEL, MEM_WIDTH).astype(BF16),
        "w_mo": w_mo.reshape(MEM_WIDTH, D_MODEL).astype(BF16), "norm_ffn": row(norm_ffn),
        "w_up": w_up.astype(BF16), "conv_w": conv_w, "conv_b": row(conv_b), "w_down": w_down.astype(BF16),
        "norm_final": row(norm_final),
    }


def kernel(x_prompt, x_sample, mem_prompt, cache_ckv, cache_krope, page_table, cache_mem_k, cache_mem_v,
           state_hgrn, state_conv, norm_mix, w_in, norm_q, norm_kv, w_uq, w_uk, w_uv, hg_lb_raw, norm_hg,
           w_out, norm_memx, norm_mem, w_mq, w_mk, w_mv, w_mo, norm_ffn, w_up, conv_w, conv_b, w_down,
           norm_final):
    bp, t, _ = x_prompt.shape
    db = x_sample.shape[0]
    n_pages = page_table.shape[1]
    page = cache_ckv.shape[2]
    past_len = n_pages * page
    w = _prep_weights(norm_mix[0], w_in[0], norm_q[0], norm_kv[0], w_uq[0], w_uk[0], w_uv[0], norm_hg[0],
                      w_out[0], norm_memx[0], norm_mem[0], w_mq[0], w_mk[0], w_mv[0], w_mo[0], norm_ffn[0],
                      w_up[0], conv_w[0], conv_b[0], w_down[0], norm_final)
    tm = ROW_TILE
    rows_p = bp * t

    mk_p, mv_p, mk_pb, mv_pb = _memkv(mem_prompt.reshape(-1, D_MODEL), w["norm_mem"], w["w_mk"], w["w_mv"])
    cos_p, sin_p = _rope_tables(jnp.arange(t, dtype=F32))
    xp = x_prompt.reshape(rows_p, D_MODEL)
    q_p, k_p, v_p, ckv_p, kr_p, zh_p = _mix_in(xp, w, cos_p, sin_p, tm)
    hq = MLA_HEADS * HEAD_PAD
    o_mla_p = _mla_prompt(q_p.reshape(bp, t, hq), k_p.reshape(bp, t, hq), v_p.reshape(bp, t, MLA_WIDTH), tm)
    o_hg_p, s_p = _hgrn_prompt(zh_p.reshape(bp, t, -1), hg_lb_raw, w["norm_hg"])
    x1_p, qm_p = _outproj(xp, o_mla_p.reshape(rows_p, -1), o_hg_p.reshape(rows_p, -1), w, tm, BF16)
    m_tok = mem_prompt.shape[1]
    om_p = _memattn_prompt(qm_p.reshape(bp, t, MEM_WIDTH), mk_pb.reshape(bp, m_tok, MEM_WIDTH),
                           mv_pb.reshape(bp, m_tok, MEM_WIDTH), tm)
    y_p, cv_p = _ffn(x1_p, om_p.reshape(rows_p, MEM_WIDTH), w, tm, seq_len=t)

    cos_s, sin_s = _rope_tables(jnp.full((db,), past_len, F32))
    xs = x_sample.reshape(db, D_MODEL)
    q_s, _, _, ckv_s, kr_s, zh_s = _mix_in(xs, w, cos_s, sin_s, db)
    qlat = jnp.transpose(_qlat(q_s, w["w_ukt"]), (1, 0, 2))
    qrope = q_s.reshape(db, MLA_HEADS, HEAD_PAD)[:, :, MLA_NOPE:MLA_NOPE + MLA_ROPE].astype(F32)
    kr_new = kr_s[:, MLA_NOPE:MLA_NOPE + MLA_ROPE]
    lat = _paged_attention(page_table, qlat, qrope, ckv_s.reshape(db, 1, KV_LORA), kr_new.reshape(db, 1, MLA_ROPE),
                           cache_ckv.reshape(-1, page, KV_LORA),
                           jnp.swapaxes(cache_krope, 2, 3).reshape(-1, MLA_ROPE, page))
    o_mla_s = _latv(jnp.transpose(lat, (1, 0, 2)), w["w_uv"])
    qcol = zh_s[:, :HG_QK].reshape(db, HG_HEADS, HG_DK, 1)
    fcol = zh_s[:, HG_QK:2 * HG_QK].reshape(db, HG_HEADS, HG_DK, 1)
    lbcol = hg_lb_raw.reshape(-1, HG_HEADS, HG_DK, 1)
    s_s, o_hg_s = _hgrn_sample(state_hgrn.reshape(db, HG_HEADS, HG_DK, HG_DV), qcol, fcol, zh_s, lbcol, w["norm_hg"])
    x1_s, qm_s = _outproj(xs, o_mla_s, o_hg_s, w, db, F32)
    om_s = _memattn_sample(qm_s.reshape(db, 1, MEM_WIDTH), cache_mem_k.reshape(db, m_tok * MEM_HEADS, MEM_HDIM),
                           cache_mem_v.reshape(db, m_tok * MEM_HEADS, MEM_HDIM))
    y_s, a_s = _ffn(x1_s, om_s.reshape(db, MEM_WIDTH), w, db, prev=(state_conv[0, :, 0], state_conv[0, :, 1]))
    cv_s = jnp.stack([state_conv[0, :, 1], a_s], axis=1)

    rope_sl = slice(MLA_NOPE, MLA_NOPE + MLA_ROPE)
    return (y_p.reshape(bp, t, D_MODEL), y_s.reshape(db, 1, D_MODEL),
            ckv_p.reshape(1, bp, t, KV_LORA), kr_p[:, rope_sl].reshape(1, bp, t, MLA_ROPE),
            mk_p.reshape(1, bp, m_tok, MEM_HEADS, MEM_HDIM), mv_p.reshape(1, bp, m_tok, MEM_HEADS, MEM_HDIM),
            s_p[None], cv_p[None],
            ckv_s.reshape(1, db, 1, KV_LORA), kr_new.reshape(1, db, 1, MLA_ROPE),
            s_s[None], cv_s[None])
```

```python
import functools

import jax
import jax.numpy as jnp
from jax import lax
from jax.experimental import pallas as pl
from jax.experimental.pallas import tpu as pltpu

F32 = jnp.float32
BF16 = jnp.bfloat16

D_MODEL = 1024
MLA_HEADS = 8
MLA_NOPE = 64
MLA_ROPE = 32
MLA_VDIM = 64
Q_LORA = 384
KV_LORA = 256
HG_HEADS = 4
HG_DK = 128
HG_DV = 128
HG_CHUNK = 32
HG_QK = HG_HEADS * HG_DK
HG_WIDTH = HG_HEADS * HG_DV
MLA_WIDTH = MLA_HEADS * MLA_VDIM
MEM_HEADS = 4
MEM_HDIM = 128
MEM_WIDTH = MEM_HEADS * MEM_HDIM
D_FF = 2816
CONV_W = 3
ROPE_BASE = 10000.0
EPS = 1e-6

LANE = 128
SUBLANE = 8
HEAD_PAD = LANE
ROPE_HALF = MLA_ROPE // 2
OFF_CKV = Q_LORA
OFF_KRT = OFF_CKV + KV_LORA
OFF_HGP = OFF_KRT + LANE
IN_COLS_PAD = OFF_HGP + 2 * HG_QK + 2 * HG_WIDTH
VMEM_LIMIT = 56 * 1024 * 1024
ROW_TILE = 512
FFN_CHUNK = D_FF // 2
PAGES_PER_CHUNK = 16
PAGES_PER_BLOCK = 8
PAGED_SLOTS = 4
HG_GROUP = 256
NEG_BIG = -1e30
LOG2_E = 1.4426950408889634


def _rms(x, g):
    return x * lax.rsqrt(jnp.mean(x * x, axis=-1, keepdims=True) + EPS) * g


def _dot(a, b):
    return jnp.dot(a, b, preferred_element_type=F32)


def _dot_nt(a, b):
    return lax.dot_general(a, b, (((1,), (1,)), ((), ())), preferred_element_type=F32)


def _dot_tn(a, b):
    return lax.dot_general(a, b, (((0,), (0,)), ((), ())), preferred_element_type=F32)


def _silu(x):
    return x * jax.nn.sigmoid(x)


def _params(*sem):
    return pltpu.CompilerParams(dimension_semantics=sem, vmem_limit_bytes=VMEM_LIMIT)


def _const_spec(shape):
    nd = len(shape)
    return pl.BlockSpec(shape, lambda *_: (0,) * nd, pipeline_mode=pl.Buffered(1))


def _memkv_kernel(mem_ref, g_ref, wk_ref, wv_ref, k_ref, v_ref, kb_ref, vb_ref):
    m = _rms(mem_ref[...], g_ref[...]).astype(BF16)
    k = _dot(m, wk_ref[...])
    v = _dot(m, wv_ref[...])
    k_ref[...] = k
    v_ref[...] = v
    kb_ref[...] = k.astype(BF16)
    vb_ref[...] = v.astype(BF16)


def _memkv(mem, g, wk, wv):
    rows = mem.shape[0]
    tm = min(ROW_TILE, rows)
    row = lambda w: pl.BlockSpec((tm, w), lambda i: (i, 0))
    return pl.pallas_call(
        _memkv_kernel,
        grid=(rows // tm,),
        in_specs=[row(D_MODEL), _const_spec((1, D_MODEL)), _const_spec((D_MODEL, MEM_WIDTH)),
                  _const_spec((D_MODEL, MEM_WIDTH))],
        out_specs=[row(MEM_WIDTH)] * 4,
        out_shape=[jax.ShapeDtypeStruct((rows, MEM_WIDTH), F32)] * 2
        + [jax.ShapeDtypeStruct((rows, MEM_WIDTH), BF16)] * 2,
        compiler_params=_params("parallel"),
        name="memkv",
    )(mem, g, wk, wv)


def _mix_in_kernel(*refs, transposed):
    if transposed:
        (x_ref, g_ref, win_ref, gq_ref, gkv_ref, wuq_ref, wuk_ref, wuv_ref, cos_ref, sin_ref, cost_ref, sint_ref,
         q_ref, k_ref, v_ref, ckv_ref, kr_ref, zh_ref) = refs
    else:
        (x_ref, g_ref, win_ref, gq_ref, gkv_ref, wuq_ref, wuk_ref, wuv_ref, cos_ref, sin_ref,
         q_ref, k_ref, v_ref, ckv_ref, kr_ref, zh_ref) = refs
    h = _rms(x_ref[...], g_ref[...]).astype(BF16)
    z = _dot(h, win_ref[...])
    c_q = _rms(z[:, :OFF_CKV], gq_ref[...]).astype(BF16)
    c_kv = _rms(z[:, OFF_CKV:OFF_KRT], gkv_ref[...])
    ckv_ref[...] = c_kv
    c_kvb = c_kv.astype(BF16)
    zh_ref[...] = z[:, OFF_HGP:]
    cos = cos_ref[...]
    sin = sin_ref[...]
    zkr = z[:, OFF_KRT:OFF_HGP]
    kr = zkr * cos + pltpu.roll(zkr, LANE - MLA_ROPE, axis=1) * sin
    kr_ref[...] = kr
    scale = (MLA_NOPE + MLA_ROPE) ** -0.5
    kn = _dot(c_kvb, wuk_ref[...])
    for hd in range(MLA_HEADS):
        sl = slice(hd * HEAD_PAD, (hd + 1) * HEAD_PAD)
        k_ref[:, sl] = (kn[:, sl] + kr).astype(BF16)
    if transposed:
        cos_q = cost_ref[...] * (scale * LOG2_E)
        sin_q = sint_ref[...] * (scale * LOG2_E)
        qt = _dot_nt(wuq_ref[...], c_q)
        for hd in range(MLA_HEADS):
            sl = slice(hd * HEAD_PAD, (hd + 1) * HEAD_PAD)
            qh = qt[sl, :]
            q_ref[sl, :] = (qh * cos_q + pltpu.roll(qh, LANE - MLA_ROPE, axis=0) * sin_q).astype(BF16)
        v_ref[...] = _dot_nt(wuv_ref[...], c_kvb).astype(BF16)
    else:
        cos_q = cos * scale
        sin_q = sin * scale
        q = _dot(c_q, wuq_ref[...])
        for hd in range(MLA_HEADS):
            sl = slice(hd * HEAD_PAD, (hd + 1) * HEAD_PAD)
            qh = q[:, sl]
            q_ref[:, sl] = (qh * cos_q + pltpu.roll(qh, LANE - MLA_ROPE, axis=1) * sin_q).astype(BF16)
        v_ref[...] = _dot(c_kvb, wuv_ref[...]).astype(BF16)


def _mix_in(x, w, cos_t, sin_t, tm, transposed):
    rows = x.shape[0]
    t_tiles = cos_t.shape[0] // tm
    row = lambda wd: pl.BlockSpec((tm, wd), lambda i: (i, 0))
    tab = pl.BlockSpec((tm, LANE), lambda i: (i % t_tiles, 0))
    hq = MLA_HEADS * HEAD_PAD
    in_specs = [row(D_MODEL), _const_spec((1, D_MODEL)), _const_spec((D_MODEL, IN_COLS_PAD)),
                _const_spec((1, Q_LORA)), _const_spec((1, KV_LORA))]
    args = [x, w["norm_mix"], w["w_in"], w["norm_q"], w["norm_kv"]]
    if transposed:
        tile_t = lambda wd: pl.BlockSpec((None, wd, tm), lambda i: (i, 0, 0))
        tab_t = pl.BlockSpec((LANE, tm), lambda i: (0, i % t_tiles))
        in_specs += [_const_spec((hq, Q_LORA)), _const_spec((KV_LORA, hq)), _const_spec((MLA_WIDTH, KV_LORA)),
                     tab, tab, tab_t, tab_t]
        args += [w["w_uq_t"], w["w_uk"], w["w_uv_t"], cos_t, sin_t, cos_t.T, sin_t.T]
        qv_specs = [tile_t(hq), row(hq), tile_t(MLA_WIDTH)]
        qv_shapes = [jax.ShapeDtypeStruct((rows // tm, hq, tm), BF16), jax.ShapeDtypeStruct((rows, hq), BF16),
                     jax.ShapeDtypeStruct((rows // tm, MLA_WIDTH, tm), BF16)]
    else:
        in_specs += [_const_spec((Q_LORA, hq)), _const_spec((KV_LORA, hq)), _const_spec((KV_LORA, MLA_WIDTH)),
                     tab, tab]
        args += [w["w_uq"], w["w_uk"], w["w_uv"], cos_t, sin_t]
        qv_specs = [row(hq), row(hq), row(MLA_WIDTH)]
        qv_shapes = [jax.ShapeDtypeStruct((rows, hq), BF16), jax.ShapeDtypeStruct((rows, hq), BF16),
                     jax.ShapeDtypeStruct((rows, MLA_WIDTH), BF16)]
    return pl.pallas_call(
        functools.partial(_mix_in_kernel, transposed=transposed),
        grid=(rows // tm,),
        in_specs=in_specs,
        out_specs=qv_specs + [row(KV_LORA), row(LANE), row(4 * HG_QK)],
        out_shape=qv_shapes + [jax.ShapeDtypeStruct((rows, KV_LORA), F32),
                               jax.ShapeDtypeStruct((rows, LANE), F32), jax.ShapeDtypeStruct((rows, 4 * HG_QK), F32)],
        compiler_params=_params("parallel"),
        name="mix_in_t" if transposed else "mix_in",
    )(*args)


HEADS_PER_STEP = 2
ATTN_KEY_TILE = 128
ATTN_QUERY_TILE = 256
ATTN_LOOKAHEAD = 8


def _mla_prompt_kernel(qt_ref, k_ref, vt_ref, o_ref, *, bq):
    qi = pl.program_id(2)
    kt_n = bq // ATTN_KEY_TILE
    qh_n = bq // ATTN_QUERY_TILE
    chains = [(hh, qh) for hh in range(HEADS_PER_STEP) for qh in range(qh_n)]

    def block(j, carry, masked):
        carry = list(carry)
        ops = [(kt * ATTN_KEY_TILE, ci) for kt in range(kt_n) for ci in range(len(chains))
               if not (masked and kt * ATTN_KEY_TILE > chains[ci][1] * ATTN_QUERY_TILE + ATTN_QUERY_TILE - 1)]

        def scores(k0, ci):
            hh, qh = chains[ci]
            q0 = qh * ATTN_QUERY_TILE
            ksl = slice(hh * HEAD_PAD, (hh + 1) * HEAD_PAD)
            rows = pl.ds(pl.multiple_of(j * bq + k0, ATTN_KEY_TILE), ATTN_KEY_TILE)
            st = _dot(k_ref[rows, ksl], qt_ref[ksl, q0:q0 + ATTN_QUERY_TILE])
            if masked and k0 + ATTN_KEY_TILE - 1 > q0:
                r = lax.broadcasted_iota(jnp.int32, st.shape, 0) + k0
                c = lax.broadcasted_iota(jnp.int32, st.shape, 1) + q0
                st = jnp.where(r <= c, st, NEG_BIG)
            return st

        pending = [scores(*op) for op in ops[:ATTN_LOOKAHEAD]]
        for i, (k0, ci) in enumerate(ops):
            if i + ATTN_LOOKAHEAD < len(ops):
                pending.append(scores(*ops[i + ATTN_LOOKAHEAD]))
            st = pending.pop(0)
            m, l, acc = carry[ci]
            hh = chains[ci][0]
            vsl = slice(hh * MLA_VDIM, (hh + 1) * MLA_VDIM)
            m_new = jnp.maximum(m, jnp.max(st, axis=0, keepdims=True))
            alpha = jnp.exp2(m - m_new)
            pt = jnp.exp2(st - m_new)
            l = alpha * l + jnp.sum(pt, axis=0, keepdims=True)
            acc = alpha * acc + _dot(vt_ref[j, vsl, k0:k0 + ATTN_KEY_TILE], pt.astype(BF16))
            carry[ci] = (m_new, l, acc)
        return tuple(carry)

    qw = ATTN_QUERY_TILE
    init = tuple((jnp.full((1, qw), NEG_BIG, F32), jnp.zeros((1, qw), F32), jnp.zeros((MLA_VDIM, qw), F32))
                 for _ in chains)
    carry = lax.fori_loop(0, qi, functools.partial(block, masked=False), init)
    carry = block(qi, carry, True)
    outs = [acc / l for _, l, acc in carry]
    heads = [jnp.concatenate(outs[hh * qh_n:(hh + 1) * qh_n], axis=1) for hh in range(HEADS_PER_STEP)]
    o_ref[...] = jnp.concatenate(heads, axis=0).T.astype(o_ref.dtype)


def _mla_prompt(qt, k, vt, b, t, bq):
    hp = HEADS_PER_STEP
    nq = t // bq
    return pl.pallas_call(
        functools.partial(_mla_prompt_kernel, bq=bq),
        grid=(b, MLA_HEADS // hp, nq),
        in_specs=[pl.BlockSpec((None, hp * HEAD_PAD, bq), lambda bi, hi, qi: (bi * nq + qi, hi, 0)),
                  pl.BlockSpec((None, t, hp * HEAD_PAD), lambda bi, hi, qi: (bi, 0, hi)),
                  pl.BlockSpec((nq, hp * MLA_VDIM, bq), lambda bi, hi, qi: (bi, hi, 0))],
        out_specs=pl.BlockSpec((None, bq, hp * MLA_VDIM), lambda bi, hi, qi: (bi, qi, hi)),
        out_shape=jax.ShapeDtypeStruct((b, t, MLA_WIDTH), BF16),
        compiler_params=_params("parallel", "parallel", "arbitrary"),
        name="mla_prompt",
    )(qt, k, vt)


def _lower_bound(raw):
    e = jnp.exp(raw - jnp.max(raw, axis=0, keepdims=True))
    return e[0:1] / jnp.sum(e, axis=0, keepdims=True)


def _hgrn_prompt_kernel(q_ref, f_ref, i_ref, gate_ref, lbraw_ref, gn_ref, o_ref, s_ref,
                        qt_scr, o_scr, b_scr, u_scr, *, t):
    c = HG_CHUNK
    g = HG_GROUP
    cpg = g // c
    lb = _lower_bound(lbraw_ref[...])
    pos = lax.broadcasted_iota(jnp.int32, (g, HG_DK), 0) % c
    r2 = lax.broadcasted_iota(jnp.int32, (g, g), 0)
    c2 = lax.broadcasted_iota(jnp.int32, (g, g), 1)
    amask = jnp.logical_and(c2 <= r2, c2 >= r2 - r2 % c)

    def group(gi, carry):
        rows = pl.ds(pl.multiple_of(gi * g, g), g)
        q = q_ref[rows, :]
        vb = i_ref[rows, :].astype(BF16)
        f = lb + (1.0 - lb) * jax.nn.sigmoid(f_ref[rows, :])
        k = 1.0 - f
        bcum = jnp.log(f)
        for sh in (1, 2, 4, 8, 16):
            bcum = bcum + jnp.where(pos >= sh, pltpu.roll(bcum, sh, axis=0), 0.0)
        b_scr[rows, :] = bcum
        q_t = (q * jnp.exp(bcum)).astype(BF16)
        k_t = (k * jnp.exp(-bcum)).astype(BF16)
        qt_scr[rows, :] = q_t
        a = _dot_nt(q_t, k_t)
        b3 = bcum.reshape(cpg, c, HG_DK)
        k_s = (k.reshape(cpg, c, HG_DK) * jnp.exp(b3[:, c - 1:c, :] - b3)).astype(BF16)
        v3 = vb.reshape(cpg, c, HG_DV)
        for ci in range(cpg):
            u_scr[gi * cpg + ci] = _dot_tn(v3[ci], k_s[ci])
        o_scr[rows, :] = _dot(jnp.where(amask, a, 0.0).astype(BF16), vb)
        return carry

    lax.fori_loop(0, t // g, group, 0, unroll=2)

    def step(ci, st):
        rows = pl.ds(pl.multiple_of(ci * c, c), c)
        o_scr[rows, :] = o_scr[rows, :] + _dot_nt(qt_scr[rows, :], st.astype(BF16))
        decay = jnp.exp(b_scr[pl.ds(ci * c + c - 1, 1), :])
        return st * decay + u_scr[ci]

    st = lax.fori_loop(0, t // c, step, jnp.zeros((HG_DV, HG_DK), F32), unroll=4)
    s_ref[...] = st.T

    gn = gn_ref[...]

    def finish(gi, carry):
        rows = pl.ds(pl.multiple_of(gi * g, g), g)
        o_ref[rows, :] = (_rms(o_scr[rows, :], gn) * _silu(gate_ref[rows, :])).astype(o_ref.dtype)
        return carry

    lax.fori_loop(0, t // g, finish, 0)


def _hgrn_prompt(zh, lb_raw, gn):
    b, t, _ = zh.shape
    col = lambda off: pl.BlockSpec((None, t, HG_DK), lambda bi, hi: (bi, 0, off + hi))
    return pl.pallas_call(
        functools.partial(_hgrn_prompt_kernel, t=t),
        grid=(b, HG_HEADS),
        in_specs=[col(0), col(HG_HEADS), col(2 * HG_HEADS), col(3 * HG_HEADS),
                  pl.BlockSpec((lb_raw.shape[0], HG_DK), lambda bi, hi: (0, hi)), _const_spec((1, HG_DV))],
        out_specs=[pl.BlockSpec((None, t, HG_DV), lambda bi, hi: (bi, 0, hi)),
                   pl.BlockSpec((None, None, HG_DK, HG_DV), lambda bi, hi: (bi, hi, 0, 0))],
        out_shape=[jax.ShapeDtypeStruct((b, t, HG_WIDTH), BF16),
                   jax.ShapeDtypeStruct((b, HG_HEADS, HG_DK, HG_DV), F32)],
        scratch_shapes=[pltpu.VMEM((t, HG_DK), BF16), pltpu.VMEM((t, HG_DV), F32), pltpu.VMEM((t, HG_DK), F32),
                        pltpu.VMEM((t // HG_CHUNK, HG_DV, HG_DK), F32)],
        compiler_params=_params("parallel", "parallel"),
        name="hgrn_prompt",
    )(zh, zh, zh, zh, lb_raw, gn)


def _qlat_kernel(q_ref, wukt_ref, o_ref):
    for hd in range(MLA_HEADS):
        o_ref[hd] = _dot(q_ref[:, hd * HEAD_PAD:(hd + 1) * HEAD_PAD], wukt_ref[hd])


def _qlat(q_pad, wukt):
    n = q_pad.shape[0]
    return pl.pallas_call(
        _qlat_kernel,
        out_shape=jax.ShapeDtypeStruct((MLA_HEADS, n, KV_LORA), F32),
        compiler_params=pltpu.CompilerParams(vmem_limit_bytes=VMEM_LIMIT),
        name="qlat",
    )(q_pad, wukt)


def _paged_kernel(pt_ref, qlat_ref, qrope_ref, ckvn_ref, krn_ref, ckv_hbm, krt_hbm, lat_ref,
                  kvbuf, krbuf, kvb, sem, *, n_pages):
    b = pl.program_id(0)
    nb = pl.num_programs(0)
    pc = PAGES_PER_CHUNK
    pb = PAGES_PER_BLOCK
    ns = PAGED_SLOTS
    ahead = ns - 1
    n_chunks = n_pages // pc
    page = kvbuf.shape[2]

    def copies(bb, ch, slot):
        out = []
        for p in range(pc):
            pg = pt_ref[bb, ch * pc + p]
            out.append(pltpu.make_async_copy(ckv_hbm.at[pg], kvbuf.at[slot, p], sem.at[0, slot]))
            out.append(pltpu.make_async_copy(krt_hbm.at[pg], krbuf.at[slot, p], sem.at[1, slot]))
        return out

    def start(bb, ch, slot):
        for cp in copies(bb, ch, slot):
            cp.start()

    @pl.when(b == 0)
    def _():
        for ch0 in range(ahead):
            start(0, ch0, ch0)

    qlat = qlat_ref[...]
    qrope = qrope_ref[...]
    qlat_b = qlat.astype(BF16)
    qrope_b = qrope.astype(BF16)

    def prefetch(ch):
        tgt = ch + ahead
        if isinstance(tgt, int) and tgt < n_chunks:
            start(b, tgt, tgt % ns)
            return
        wrap = tgt >= n_chunks
        tb = jnp.minimum(b + jnp.where(wrap, 1, 0), nb - 1)
        start(tb, jnp.where(wrap, tgt - n_chunks, tgt), tgt % ns)

    def load_and_score(ch):
        slot = ch % ns
        for cp in copies(b, ch, slot):
            cp.wait()
        par = ch % 2
        scores = []
        for j in range(pc // pb):
            kv = kvbuf[slot, pl.ds(j * pb, pb)].reshape(pb * page, KV_LORA).astype(BF16)
            kvb[par, pl.ds(j * pb * page, pb * page), :] = kv
            krt = jnp.concatenate([krbuf[slot, j * pb + p] for p in range(pb)], axis=1).astype(BF16)
            scores.append(_dot_nt(qlat_b, kv) + _dot(qrope_b, krt))
        return jnp.concatenate(scores, axis=1)

    def attend(ch, s, m, l, acc):
        m_new = jnp.maximum(m, jnp.max(s, axis=-1, keepdims=True))
        alpha = jnp.exp(m - m_new)
        p = jnp.exp(s - m_new)
        l = alpha * l + jnp.sum(p, axis=-1, keepdims=True)
        acc = alpha * acc + _dot(p.astype(BF16), kvb[ch % 2])
        return m_new, l, acc

    def chunk(ch, carry):
        s, m, l, acc = carry
        prefetch(ch + 1)
        s_next = load_and_score(ch + 1)
        m, l, acc = attend(ch, s, m, l, acc)
        return s_next, m, l, acc

    prefetch(0)
    init = (load_and_score(0), jnp.full((MLA_HEADS, 1), NEG_BIG, F32), jnp.zeros((MLA_HEADS, 1), F32),
            jnp.zeros((MLA_HEADS, KV_LORA), F32))
    s, m, l, acc = lax.fori_loop(0, n_chunks - 1, chunk, init)
    m, l, acc = attend(n_chunks - 1, s, m, l, acc)

    @pl.when(b == nb - 1)
    def _():
        for i in range(ahead):
            for cp in copies(b, i, (n_chunks + i) % ns):
                cp.wait()

    ckvn = ckvn_ref[...]
    s_new = (jnp.sum(qlat * ckvn, axis=-1, keepdims=True)
             + jnp.sum(qrope * krn_ref[...], axis=-1, keepdims=True))
    m_new = jnp.maximum(m, s_new)
    alpha = jnp.exp(m - m_new)
    p_new = jnp.exp(s_new - m_new)
    l = alpha * l + p_new
    lat_ref[...] = (alpha * acc + p_new * ckvn) / l


def _paged_attention(page_table, qlat, qrope, ckv_new, kr_new, cache_ckv, cache_kr):
    n, n_pages = page_table.shape
    page = cache_ckv.shape[1]
    pc = PAGES_PER_CHUNK
    per = lambda shape: pl.BlockSpec((None,) + shape, lambda bi, pt: (bi, 0, 0))
    grid_spec = pltpu.PrefetchScalarGridSpec(
        num_scalar_prefetch=1,
        grid=(n,),
        in_specs=[per((MLA_HEADS, KV_LORA)), per((MLA_HEADS, MLA_ROPE)), per((1, KV_LORA)), per((1, MLA_ROPE)),
                  pl.BlockSpec(memory_space=pl.ANY), pl.BlockSpec(memory_space=pl.ANY)],
        out_specs=per((MLA_HEADS, KV_LORA)),
        scratch_shapes=[pltpu.VMEM((PAGED_SLOTS, pc, page, KV_LORA), F32),
                        pltpu.VMEM((PAGED_SLOTS, pc, MLA_ROPE, page), F32),
                        pltpu.VMEM((2, pc * page, KV_LORA), BF16),
                        pltpu.SemaphoreType.DMA((2, PAGED_SLOTS))],
    )
    return pl.pallas_call(
        functools.partial(_paged_kernel, n_pages=n_pages),
        grid_spec=grid_spec,
        out_shape=jax.ShapeDtypeStruct((n, MLA_HEADS, KV_LORA), F32),
        compiler_params=_params("arbitrary"),
        name="paged_mla",
    )(page_table, qlat, qrope, ckv_new, kr_new, cache_ckv, cache_kr)


def _latv_kernel(lat_ref, wuv_ref, o_ref):
    outs = [_dot(lat_ref[hd].astype(BF16), wuv_ref[:, hd * MLA_VDIM:(hd + 1) * MLA_VDIM])
            for hd in range(MLA_HEADS)]
    o_ref[...] = jnp.concatenate(outs, axis=-1).astype(o_ref.dtype)


def _latv(lat_hb, wuv):
    n = lat_hb.shape[1]
    return pl.pallas_call(
        _latv_kernel,
        out_shape=jax.ShapeDtypeStruct((n, MLA_WIDTH), BF16),
        compiler_params=pltpu.CompilerParams(vmem_limit_bytes=VMEM_LIMIT),
        name="latv",
    )(lat_hb, wuv)


HG_SAMPLE_BLOCK = 8


def _hgrn_sample_kernel(s_ref, qcol_ref, fcol_ref, zh_ref, lbcol_ref, gn_ref, so_ref, o_ref):
    gn = gn_ref[...]
    rows = []
    for j in range(HG_SAMPLE_BLOCK):
        heads = []
        for hd in range(HG_HEADS):
            lb = _lower_bound(lbcol_ref[:, hd])
            lb = lb[0]
            f = lb + (1.0 - lb) * jax.nn.sigmoid(fcol_ref[j, hd])
            k = 1.0 - f
            v = zh_ref[j:j + 1, 2 * HG_QK + hd * HG_DV:2 * HG_QK + (hd + 1) * HG_DV]
            s_new = f * s_ref[j, hd] + k * v
            so_ref[j, hd] = s_new
            o = jnp.sum(qcol_ref[j, hd] * s_new, axis=0, keepdims=True)
            heads.append(_rms(o, gn))
        rows.append(jnp.concatenate(heads, axis=-1))
    o = jnp.concatenate(rows, axis=0)
    gate = zh_ref[:, 2 * HG_QK + HG_WIDTH:]
    o_ref[...] = (o * _silu(gate)).astype(o_ref.dtype)


def _hgrn_sample(state, qcol, fcol, zh, lbcol, gn):
    n = state.shape[0]
    nb = HG_SAMPLE_BLOCK
    blk4 = lambda last: pl.BlockSpec((nb, HG_HEADS, HG_DK, last), lambda i: (i, 0, 0, 0))
    return pl.pallas_call(
        _hgrn_sample_kernel,
        grid=(n // nb,),
        in_specs=[blk4(HG_DV), blk4(1), blk4(1), pl.BlockSpec((nb, zh.shape[1]), lambda i: (i, 0)),
                  _const_spec(lbcol.shape), _const_spec((1, HG_DV))],
        out_specs=[blk4(HG_DV), pl.BlockSpec((nb, HG_WIDTH), lambda i: (i, 0))],
        out_shape=[jax.ShapeDtypeStruct(state.shape, F32), jax.ShapeDtypeStruct((n, HG_WIDTH), BF16)],
        compiler_params=_params("parallel"),
        name="hgrn_sample",
    )(state, qcol, fcol, zh, lbcol, gn)


def _outproj_kernel(x_ref, omla_ref, ohg_ref, wout_ref, g_ref, wmq_ref, x1_ref, qm_ref):
    x1 = (x_ref[...] + _dot(omla_ref[...], wout_ref[:MLA_WIDTH, :])
          + _dot(ohg_ref[...], wout_ref[MLA_WIDTH:, :]))
    x1_ref[...] = x1
    hm = _rms(x1, g_ref[...]).astype(BF16)
    qm_ref[...] = (_dot(hm, wmq_ref[...]) * (MEM_HDIM ** -0.5)).astype(qm_ref.dtype)


def _outproj(x, omla, ohg, w, tm, q_dtype):
    rows = x.shape[0]
    row = lambda wd: pl.BlockSpec((tm, wd), lambda i: (i, 0))
    return pl.pallas_call(
        _outproj_kernel,
        grid=(rows // tm,),
        in_specs=[row(D_MODEL), row(MLA_WIDTH), row(HG_WIDTH), _const_spec((MLA_WIDTH + HG_WIDTH, D_MODEL)),
                  _const_spec((1, D_MODEL)), _const_spec((D_MODEL, MEM_WIDTH))],
        out_specs=[row(D_MODEL), row(MEM_WIDTH)],
        out_shape=[jax.ShapeDtypeStruct((rows, D_MODEL), F32), jax.ShapeDtypeStruct((rows, MEM_WIDTH), q_dtype)],
        compiler_params=_params("parallel"),
        name="outproj",
    )(x, omla, ohg, w["w_out"], w["norm_memx"], w["w_mq"])


def _memattn_prompt_kernel(q_ref, k_ref, v_ref, o_ref):
    outs = []
    for hd in range(MEM_HEADS):
        sl = slice(hd * MEM_HDIM, (hd + 1) * MEM_HDIM)
        s = _dot_nt(q_ref[:, sl], k_ref[:, sl])
        p = jnp.exp(s - jnp.max(s, axis=-1, keepdims=True))
        o = _dot(p.astype(BF16), v_ref[:, sl])
        outs.append(o / jnp.sum(p, axis=-1, keepdims=True))
    o_ref[...] = jnp.concatenate(outs, axis=-1).astype(o_ref.dtype)


def _memattn_prompt(q, k, v, tm):
    b, t, _ = q.shape
    m = k.shape[1]
    return pl.pallas_call(
        _memattn_prompt_kernel,
        grid=(b, t // tm),
        in_specs=[pl.BlockSpec((None, tm, MEM_WIDTH), lambda bi, ti: (bi, ti, 0)),
                  pl.BlockSpec((None, m, MEM_WIDTH), lambda bi, ti: (bi, 0, 0)),
                  pl.BlockSpec((None, m, MEM_WIDTH), lambda bi, ti: (bi, 0, 0))],
        out_specs=pl.BlockSpec((None, tm, MEM_WIDTH), lambda bi, ti: (bi, ti, 0)),
        out_shape=jax.ShapeDtypeStruct((b, t, MEM_WIDTH), BF16),
        compiler_params=_params("parallel", "parallel"),
        name="memattn_prompt",
    )(q, k, v)


def _memattn_sample_kernel(q_ref, k_ref, v_ref, o_ref):
    rows = k_ref.shape[0]
    q = jnp.concatenate([q_ref[:, hd * MEM_HDIM:(hd + 1) * MEM_HDIM] for hd in range(MEM_HEADS)]
                        + [jnp.zeros((SUBLANE - MEM_HEADS, MEM_HDIM), F32)], axis=0)
    s = _dot_nt(q.astype(BF16), k_ref[...].astype(BF16))
    hid = lax.broadcasted_iota(jnp.int32, (SUBLANE, rows), 0)
    rhd = lax.broadcasted_iota(jnp.int32, (SUBLANE, rows), 1) % MEM_HEADS
    s = jnp.where(hid == rhd, s, NEG_BIG)
    p = jnp.exp(s - jnp.max(s, axis=-1, keepdims=True))
    p = jnp.where(hid == rhd, p, 0.0)
    p = p / jnp.sum(p, axis=-1, keepdims=True)
    of = _dot(p.astype(BF16), v_ref[...].astype(BF16))
    o_ref[...] = jnp.concatenate([of[hd:hd + 1] for hd in range(MEM_HEADS)], axis=-1).astype(o_ref.dtype)


def _memattn_sample(q, k, v):
    n, m = k.shape[:2]
    return pl.pallas_call(
        _memattn_sample_kernel,
        grid=(n,),
        in_specs=[pl.BlockSpec((None, 1, MEM_WIDTH), lambda i: (i, 0, 0)),
                  pl.BlockSpec((None, m, MEM_HDIM), lambda i: (i, 0, 0)),
                  pl.BlockSpec((None, m, MEM_HDIM), lambda i: (i, 0, 0))],
        out_specs=pl.BlockSpec((None, 1, MEM_WIDTH), lambda i: (i, 0, 0)),
        out_shape=jax.ShapeDtypeStruct((n, 1, MEM_WIDTH), BF16),
        compiler_params=_params("parallel"),
        name="memattn_sample",
    )(q, k, v)


def _ffn_kernel(*refs, seq_mode, tiles_per_seq):
    if seq_mode:
        (x1_ref, om_ref, wmo_ref, g_ref, wup_ref, cw_ref, cb_ref, wdn_ref, gfin_ref,
         y_ref, tail_ref, carry_ref) = refs
    else:
        (x1_ref, om_ref, wmo_ref, g_ref, wup_ref, cw_ref, cb_ref, wdn_ref, gfin_ref, prev2_ref, prev1_ref,
         y_ref, tail_ref) = refs
    tm = x1_ref.shape[0]
    x2 = x1_ref[...] + _dot(om_ref[...], wmo_ref[...])
    h = _rms(x2, g_ref[...]).astype(BF16)
    acc = x2
    if seq_mode:
        first = pl.program_id(0) % tiles_per_seq == 0
        rid = lax.broadcasted_iota(jnp.int32, (tm, 1), 0)

        @pl.when(pl.program_id(0) == 0)
        def _():
            carry_ref[...] = jnp.zeros_like(carry_ref)
    for f0 in range(0, D_FF, FFN_CHUNK):
        fs = slice(f0, f0 + FFN_CHUNK)
        a = _dot(h, wup_ref[:, fs])
        gt = _dot(h, wup_ref[:, D_FF + f0:D_FF + f0 + FFN_CHUNK])
        if seq_mode:
            prev = jnp.where(first, 0.0, carry_ref[:, fs])
            p1 = prev[SUBLANE - 1:SUBLANE]
            p2 = prev[SUBLANE - 2:SUBLANE - 1]
            a1 = jnp.where(rid == 0, p1, pltpu.roll(a, 1, axis=0))
            a2 = jnp.where(rid == 0, p2, jnp.where(rid == 1, p1, pltpu.roll(a, 2, axis=0)))
            carry_ref[:, fs] = a[tm - SUBLANE:, :]
            tail_ref[:, fs] = a[tm - (CONV_W - 1):, :]
        else:
            a1 = prev1_ref[:, fs]
            a2 = prev2_ref[:, fs]
            tail_ref[:, fs] = a
        conv = cb_ref[:, fs] + a2 * cw_ref[0:1, fs] + a1 * cw_ref[1:2, fs] + a * cw_ref[2:3, fs]
        u = (_silu(conv) * gt).astype(BF16)
        acc = acc + _dot(u, wdn_ref[fs, :])
    y_ref[...] = _rms(acc, gfin_ref[...])


def _ffn(x1, om, w, tm, seq_len=None, prev=None):
    rows = x1.shape[0]
    seq_mode = prev is None
    row = lambda wd: pl.BlockSpec((tm, wd), lambda i: (i, 0))
    in_specs = [row(D_MODEL), row(MEM_WIDTH), _const_spec((MEM_WIDTH, D_MODEL)), _const_spec((1, D_MODEL)),
                _const_spec((D_MODEL, 2 * D_FF)), _const_spec((CONV_W, D_FF)), _const_spec((1, D_FF)),
                _const_spec((D_FF, D_MODEL)), _const_spec((1, D_MODEL))]
    args = [x1, om, w["w_mo"], w["norm_ffn"], w["w_up"], w["conv_w"], w["conv_b"], w["w_down"], w["norm_final"]]
    if seq_mode:
        tiles_per_seq = seq_len // tm
        n_seq = rows // seq_len
        tail_spec = pl.BlockSpec((None, CONV_W - 1, D_FF), lambda i: (i // tiles_per_seq, 0, 0))
        tail_shape = jax.ShapeDtypeStruct((n_seq, CONV_W - 1, D_FF), F32)
        scratch = [pltpu.VMEM((SUBLANE, D_FF), F32)]
        sem = "arbitrary"
    else:
        tiles_per_seq = 1
        in_specs += [row(D_FF), row(D_FF)]
        args += list(prev)
        tail_spec = row(D_FF)
        tail_shape = jax.ShapeDtypeStruct((rows, D_FF), F32)
        scratch = []
        sem = "parallel"
    return pl.pallas_call(
        functools.partial(_ffn_kernel, seq_mode=seq_mode, tiles_per_seq=tiles_per_seq),
        grid=(rows // tm,),
        in_specs=in_specs,
        out_specs=[row(D_MODEL), tail_spec],
        out_shape=[jax.ShapeDtypeStruct((rows, D_MODEL), F32), tail_shape],
        scratch_shapes=scratch,
        compiler_params=_params(sem),
        name="ffn_seq" if seq_mode else "ffn_step",
    )(*args)


def _rope_tables(pos):
    half = ROPE_HALF
    inv = ROPE_BASE ** (-jnp.arange(half, dtype=F32) / half)
    ang = pos[:, None] * inv[None, :]
    cos, sin = jnp.cos(ang), jnp.sin(ang)
    n = pos.shape[0]
    one = jnp.ones((n, MLA_NOPE), F32)
    zero = jnp.zeros((n, MLA_NOPE), F32)
    pad = jnp.zeros((n, LANE - MLA_NOPE - MLA_ROPE), F32)
    cos_t = jnp.concatenate([one, cos, cos, pad], axis=-1)
    sin_t = jnp.concatenate([zero, -sin, sin, pad], axis=-1)
    return cos_t, sin_t


def _rope_tile(w_rope):
    x1, x2 = w_rope[..., :ROPE_HALF], w_rope[..., ROPE_HALF:]
    return jnp.concatenate([x1, x2, x2, x1], axis=-1)


def _prep_weights(norm_mix, w_in, norm_q, norm_kv, w_uq, w_uk, w_uv, norm_hg, w_out, norm_memx, norm_mem,
                  w_mq, w_mk, w_mv, w_mo, norm_ffn, w_up, conv_w, conv_b, w_down, norm_final):
    off_kr = Q_LORA + KV_LORA
    off_hg = off_kr + MLA_ROPE
    kr_tile = jnp.concatenate([jnp.zeros((D_MODEL, MLA_NOPE), F32), _rope_tile(w_in[:, off_kr:off_hg])], axis=-1)
    w_in_pad = jnp.concatenate([w_in[:, :off_kr], kr_tile, w_in[:, off_hg:]], axis=-1)
    uq = jnp.concatenate([w_uq[..., :MLA_NOPE], _rope_tile(w_uq[..., MLA_NOPE:])], axis=-1)
    uk = jnp.concatenate([w_uk, jnp.zeros((KV_LORA, MLA_HEADS, HEAD_PAD - MLA_NOPE), F32)], axis=-1)
    ukt = jnp.transpose(uk, (1, 2, 0))
    row = lambda g: g.reshape(1, -1)
    return {
        "norm_mix": row(norm_mix), "w_in": w_in_pad.astype(BF16), "norm_q": row(norm_q), "norm_kv": row(norm_kv),
        "w_uq": uq.reshape(Q_LORA, MLA_HEADS * HEAD_PAD).astype(BF16),
        "w_uq_t": uq.reshape(Q_LORA, MLA_HEADS * HEAD_PAD).T.astype(BF16),
        "w_uv_t": w_uv.reshape(KV_LORA, MLA_WIDTH).T.astype(BF16),
        "w_uk": uk.reshape(KV_LORA, MLA_HEADS * HEAD_PAD).astype(BF16),
        "w_ukt": ukt.astype(BF16),
        "w_uv": w_uv.reshape(KV_LORA, MLA_WIDTH).astype(BF16),
        "norm_hg": row(norm_hg), "w_out": w_out.astype(BF16), "norm_memx": row(norm_memx),
        "norm_mem": row(norm_mem), "w_mq": w_mq.reshape(D_MODEL, MEM_WIDTH).astype(BF16),
        "w_mk": w_mk.reshape(D_MODEL, MEM_WIDTH).astype(BF16), "w_mv": w_mv.reshape(D_MODEL, MEM_WIDTH).astype(BF16),
        "w_mo": w_mo.reshape(MEM_WIDTH, D_MODEL).astype(BF16), "norm_ffn": row(norm_ffn),
        "w_up": w_up.astype(BF16), "conv_w": conv_w, "conv_b": row(conv_b), "w_down": w_down.astype(BF16),
        "norm_final": row(norm_final),
    }


def kernel(x_prompt, x_sample, mem_prompt, cache_ckv, cache_krope, page_table, cache_mem_k, cache_mem_v,
           state_hgrn, state_conv, norm_mix, w_in, norm_q, norm_kv, w_uq, w_uk, w_uv, hg_lb_raw, norm_hg,
           w_out, norm_memx, norm_mem, w_mq, w_mk, w_mv, w_mo, norm_ffn, w_up, conv_w, conv_b, w_down,
           norm_final):
    bp, t, _ = x_prompt.shape
    db = x_sample.shape[0]
    n_pages = page_table.shape[1]
    page = cache_ckv.shape[2]
    past_len = n_pages * page
    w = _prep_weights(norm_mix[0], w_in[0], norm_q[0], norm_kv[0], w_uq[0], w_uk[0], w_uv[0], norm_hg[0],
                      w_out[0], norm_memx[0], norm_mem[0], w_mq[0], w_mk[0], w_mv[0], w_mo[0], norm_ffn[0],
                      w_up[0], conv_w[0], conv_b[0], w_down[0], norm_final)
    tm = ROW_TILE
    rows_p = bp * t

    mk_p, mv_p, mk_pb, mv_pb = _memkv(mem_prompt.reshape(-1, D_MODEL), w["norm_mem"], w["w_mk"], w["w_mv"])
    cos_p, sin_p = _rope_tables(jnp.arange(t, dtype=F32))
    xp = x_prompt.reshape(rows_p, D_MODEL)
    qt_p, k_p, vt_p, ckv_p, kr_p, zh_p = _mix_in(xp, w, cos_p, sin_p, tm, True)
    hq = MLA_HEADS * HEAD_PAD
    o_mla_p = _mla_prompt(qt_p, k_p.reshape(bp, t, hq), vt_p, bp, t, tm)
    o_hg_p, s_p = _hgrn_prompt(zh_p.reshape(bp, t, -1), hg_lb_raw, w["norm_hg"])
    x1_p, qm_p = _outproj(xp, o_mla_p.reshape(rows_p, -1), o_hg_p.reshape(rows_p, -1), w, tm, BF16)
    m_tok = mem_prompt.shape[1]
    om_p = _memattn_prompt(qm_p.reshape(bp, t, MEM_WIDTH), mk_pb.reshape(bp, m_tok, MEM_WIDTH),
                           mv_pb.reshape(bp, m_tok, MEM_WIDTH), tm)
    y_p, cv_p = _ffn(x1_p, om_p.reshape(rows_p, MEM_WIDTH), w, tm, seq_len=t)

    cos_s, sin_s = _rope_tables(jnp.full((db,), past_len, F32))
    xs = x_sample.reshape(db, D_MODEL)
    q_s, _, _, ckv_s, kr_s, zh_s = _mix_in(xs, w, cos_s, sin_s, db, False)
    qlat = jnp.transpose(_qlat(q_s, w["w_ukt"]), (1, 0, 2))
    qrope = q_s.reshape(db, MLA_HEADS, HEAD_PAD)[:, :, MLA_NOPE:MLA_NOPE + MLA_ROPE].astype(F32)
    kr_new = kr_s[:, MLA_NOPE:MLA_NOPE + MLA_ROPE]
    lat = _paged_attention(page_table, qlat, qrope, ckv_s.reshape(db, 1, KV_LORA), kr_new.reshape(db, 1, MLA_ROPE),
                           cache_ckv.reshape(-1, page, KV_LORA),
                           jnp.swapaxes(cache_krope, 2, 3).reshape(-1, MLA_ROPE, page))
    o_mla_s = _latv(jnp.transpose(lat, (1, 0, 2)), w["w_uv"])
    qcol = zh_s[:, :HG_QK].reshape(db, HG_HEADS, HG_DK, 1)
    fcol = zh_s[:, HG_QK:2 * HG_QK].reshape(db, HG_HEADS, HG_DK, 1)
    lbcol = hg_lb_raw.reshape(-1, HG_HEADS, HG_DK, 1)
    s_s, o_hg_s = _hgrn_sample(state_hgrn.reshape(db, HG_HEADS, HG_DK, HG_DV), qcol, fcol, zh_s, lbcol, w["norm_hg"])
    x1_s, qm_s = _outproj(xs, o_mla_s, o_hg_s, w, db, F32)
    om_s = _memattn_sample(qm_s.reshape(db, 1, MEM_WIDTH), cache_mem_k.reshape(db, m_tok * MEM_HEADS, MEM_HDIM),
                           cache_mem_v.reshape(db, m_tok * MEM_HEADS, MEM_HDIM))
    y_s, a_s = _ffn(x1_s, om_s.reshape(db, MEM_WIDTH), w, db, prev=(state_conv[0, :, 0], state_conv[0, :, 1]))
    cv_s = jnp.stack([state_conv[0, :, 1], a_s], axis=1)

    rope_sl = slice(MLA_NOPE, MLA_NOPE + MLA_ROPE)
    return (y_p.reshape(bp, t, D_MODEL), y_s.reshape(db, 1, D_MODEL),
            ckv_p.reshape(1, bp, t, KV_LORA), kr_p[:, rope_sl].reshape(1, bp, t, MLA_ROPE),
            mk_p.reshape(1, bp, m_tok, MEM_HEADS, MEM_HDIM), mv_p.reshape(1, bp, m_tok, MEM_HEADS, MEM_HDIM),
            s_p[None], cv_p[None],
            ckv_s.reshape(1, db, 1, KV_LORA), kr_new.reshape(1, db, 1, MLA_ROPE),
            s_s[None], cv_s[None])
```

```python
import functools

import jax
import jax.numpy as jnp
from jax import lax
from jax.experimental import pallas as pl
from jax.experimental.pallas import tpu as pltpu

F32 = jnp.float32
BF16 = jnp.bfloat16

D_MODEL = 1024
MLA_HEADS = 8
MLA_NOPE = 64
MLA_ROPE = 32
MLA_VDIM = 64
Q_LORA = 384
KV_LORA = 256
HG_HEADS = 4
HG_DK = 128
HG_DV = 128
HG_CHUNK = 32
HG_QK = HG_HEADS * HG_DK
HG_WIDTH = HG_HEADS * HG_DV
MLA_WIDTH = MLA_HEADS * MLA_VDIM
MEM_HEADS = 4
MEM_HDIM = 128
MEM_WIDTH = MEM_HEADS * MEM_HDIM
D_FF = 2816
CONV_W = 3
ROPE_BASE = 10000.0
EPS = 1e-6

LANE = 128
SUBLANE = 8
HEAD_PAD = LANE
ROPE_HALF = MLA_ROPE // 2
OFF_CKV = Q_LORA
OFF_KRT = OFF_CKV + KV_LORA
OFF_HGP = OFF_KRT + LANE
IN_COLS_PAD = OFF_HGP + 2 * HG_QK + 2 * HG_WIDTH
VMEM_LIMIT = 56 * 1024 * 1024
ROW_TILE = 512
FFN_CHUNK = D_FF // 2
PAGES_PER_CHUNK = 16
PAGES_PER_BLOCK = 8
PAGED_SLOTS = 4
HG_GROUP = 256
NEG_BIG = -1e30
LOG2_E = 1.4426950408889634


def _rms(x, g):
    return x * lax.rsqrt(jnp.mean(x * x, axis=-1, keepdims=True) + EPS) * g


def _dot(a, b):
    return jnp.dot(a, b, preferred_element_type=F32)


def _dot_nt(a, b):
    return lax.dot_general(a, b, (((1,), (1,)), ((), ())), preferred_element_type=F32)


def _dot_tn(a, b):
    return lax.dot_general(a, b, (((0,), (0,)), ((), ())), preferred_element_type=F32)


def _silu(x):
    return x * jax.nn.sigmoid(x)


def _params(*sem):
    return pltpu.CompilerParams(dimension_semantics=sem, vmem_limit_bytes=VMEM_LIMIT)


def _const_spec(shape):
    nd = len(shape)
    return pl.BlockSpec(shape, lambda *_: (0,) * nd, pipeline_mode=pl.Buffered(1))


def _memkv_kernel(mem_ref, g_ref, wk_ref, wv_ref, k_ref, v_ref, kb_ref, vb_ref):
    m = _rms(mem_ref[...], g_ref[...]).astype(BF16)
    k = _dot(m, wk_ref[...])
    v = _dot(m, wv_ref[...])
    k_ref[...] = k
    v_ref[...] = v
    kb_ref[...] = k.astype(BF16)
    vb_ref[...] = v.astype(BF16)


def _memkv(mem, g, wk, wv):
    rows = mem.shape[0]
    tm = min(ROW_TILE, rows)
    row = lambda w: pl.BlockSpec((tm, w), lambda i: (i, 0))
    return pl.pallas_call(
        _memkv_kernel,
        grid=(rows // tm,),
        in_specs=[row(D_MODEL), _const_spec((1, D_MODEL)), _const_spec((D_MODEL, MEM_WIDTH)),
                  _const_spec((D_MODEL, MEM_WIDTH))],
        out_specs=[row(MEM_WIDTH)] * 4,
        out_shape=[jax.ShapeDtypeStruct((rows, MEM_WIDTH), F32)] * 2
        + [jax.ShapeDtypeStruct((rows, MEM_WIDTH), BF16)] * 2,
        compiler_params=_params("parallel"),
        name="memkv",
    )(mem, g, wk, wv)


def _mix_in_kernel(*refs, transposed):
    if transposed:
        (x_ref, g_ref, win_ref, gq_ref, gkv_ref, wuq_ref, wuk_ref, wuv_ref, cos_ref, sin_ref, cost_ref, sint_ref,
         q_ref, k_ref, v_ref, ckv_ref, kr_ref, zh_ref) = refs
    else:
        (x_ref, g_ref, win_ref, gq_ref, gkv_ref, wuq_ref, wuk_ref, wuv_ref, cos_ref, sin_ref,
         q_ref, k_ref, v_ref, ckv_ref, kr_ref, zh_ref) = refs
    h = _rms(x_ref[...], g_ref[...]).astype(BF16)
    z = _dot(h, win_ref[...])
    c_q = _rms(z[:, :OFF_CKV], gq_ref[...]).astype(BF16)
    c_kv = _rms(z[:, OFF_CKV:OFF_KRT], gkv_ref[...])
    ckv_ref[...] = c_kv
    c_kvb = c_kv.astype(BF16)
    zh_ref[...] = z[:, OFF_HGP:]
    cos = cos_ref[...]
    sin = sin_ref[...]
    zkr = z[:, OFF_KRT:OFF_HGP]
    kr = zkr * cos + pltpu.roll(zkr, LANE - MLA_ROPE, axis=1) * sin
    kr_ref[...] = kr
    scale = (MLA_NOPE + MLA_ROPE) ** -0.5
    kn = _dot(c_kvb, wuk_ref[...])
    for hd in range(MLA_HEADS):
        sl = slice(hd * HEAD_PAD, (hd + 1) * HEAD_PAD)
        k_ref[:, sl] = (kn[:, sl] + kr).astype(BF16)
    if transposed:
        cos_q = cost_ref[...] * (scale * LOG2_E)
        sin_q = sint_ref[...] * (scale * LOG2_E)
        qt = _dot_nt(wuq_ref[...], c_q)
        for hd in range(MLA_HEADS):
            sl = slice(hd * HEAD_PAD, (hd + 1) * HEAD_PAD)
            qh = qt[sl, :]
            q_ref[sl, :] = (qh * cos_q + pltpu.roll(qh, LANE - MLA_ROPE, axis=0) * sin_q).astype(BF16)
        v_ref[...] = _dot_nt(wuv_ref[...], c_kvb).astype(BF16)
    else:
        cos_q = cos * scale
        sin_q = sin * scale
        q = _dot(c_q, wuq_ref[...])
        for hd in range(MLA_HEADS):
            sl = slice(hd * HEAD_PAD, (hd + 1) * HEAD_PAD)
            qh = q[:, sl]
            q_ref[:, sl] = (qh * cos_q + pltpu.roll(qh, LANE - MLA_ROPE, axis=1) * sin_q).astype(BF16)
        v_ref[...] = _dot(c_kvb, wuv_ref[...]).astype(BF16)


def _mix_in(x, w, cos_t, sin_t, tm, transposed):
    rows = x.shape[0]
    t_tiles = cos_t.shape[0] // tm
    row = lambda wd: pl.BlockSpec((tm, wd), lambda i: (i, 0))
    tab = pl.BlockSpec((tm, LANE), lambda i: (i % t_tiles, 0))
    hq = MLA_HEADS * HEAD_PAD
    in_specs = [row(D_MODEL), _const_spec((1, D_MODEL)), _const_spec((D_MODEL, IN_COLS_PAD)),
                _const_spec((1, Q_LORA)), _const_spec((1, KV_LORA))]
    args = [x, w["norm_mix"], w["w_in"], w["norm_q"], w["norm_kv"]]
    if transposed:
        tile_t = lambda wd: pl.BlockSpec((None, wd, tm), lambda i: (i, 0, 0))
        tab_t = pl.BlockSpec((LANE, tm), lambda i: (0, i % t_tiles))
        in_specs += [_const_spec((hq, Q_LORA)), _const_spec((KV_LORA, hq)), _const_spec((MLA_WIDTH, KV_LORA)),
                     tab, tab, tab_t, tab_t]
        args += [w["w_uq_t"], w["w_uk"], w["w_uv_t"], cos_t, sin_t, cos_t.T, sin_t.T]
        qv_specs = [tile_t(hq), row(hq), tile_t(MLA_WIDTH)]
        qv_shapes = [jax.ShapeDtypeStruct((rows // tm, hq, tm), BF16), jax.ShapeDtypeStruct((rows, hq), BF16),
                     jax.ShapeDtypeStruct((rows // tm, MLA_WIDTH, tm), BF16)]
    else:
        in_specs += [_const_spec((Q_LORA, hq)), _const_spec((KV_LORA, hq)), _const_spec((KV_LORA, MLA_WIDTH)),
                     tab, tab]
        args += [w["w_uq"], w["w_uk"], w["w_uv"], cos_t, sin_t]
        qv_specs = [row(hq), row(hq), row(MLA_WIDTH)]
        qv_shapes = [jax.ShapeDtypeStruct((rows, hq), BF16), jax.ShapeDtypeStruct((rows, hq), BF16),
                     jax.ShapeDtypeStruct((rows, MLA_WIDTH), BF16)]
    return pl.pallas_call(
        functools.partial(_mix_in_kernel, transposed=transposed),
        grid=(rows // tm,),
        in_specs=in_specs,
        out_specs=qv_specs + [row(KV_LORA), row(LANE), row(4 * HG_QK)],
        out_shape=qv_shapes + [jax.ShapeDtypeStruct((rows, KV_LORA), F32),
                               jax.ShapeDtypeStruct((rows, LANE), F32), jax.ShapeDtypeStruct((rows, 4 * HG_QK), F32)],
        compiler_params=_params("parallel"),
        name="mix_in_t" if transposed else "mix_in",
    )(*args)


HEADS_PER_STEP = 2
ATTN_KEY_TILE = 128
ATTN_QUERY_TILE = 256
ATTN_LOOKAHEAD = 8


def _mla_prompt_kernel(qt_ref, k_ref, vt_ref, o_ref, *, bq):
    qi = pl.program_id(2)
    kt_n = bq // ATTN_KEY_TILE
    qh_n = bq // ATTN_QUERY_TILE
    chains = [(hh, qh) for hh in range(HEADS_PER_STEP) for qh in range(qh_n)]

    def block(j, carry, masked):
        carry = list(carry)
        ops = [(kt * ATTN_KEY_TILE, ci) for kt in range(kt_n) for ci in range(len(chains))
               if not (masked and kt * ATTN_KEY_TILE > chains[ci][1] * ATTN_QUERY_TILE + ATTN_QUERY_TILE - 1)]

        def scores(k0, ci):
            hh, qh = chains[ci]
            q0 = qh * ATTN_QUERY_TILE
            ksl = slice(hh * HEAD_PAD, (hh + 1) * HEAD_PAD)
            rows = pl.ds(pl.multiple_of(j * bq + k0, ATTN_KEY_TILE), ATTN_KEY_TILE)
            st = _dot(k_ref[rows, ksl], qt_ref[ksl, q0:q0 + ATTN_QUERY_TILE])
            if masked and k0 + ATTN_KEY_TILE - 1 > q0:
                r = lax.broadcasted_iota(jnp.int32, st.shape, 0) + k0
                c = lax.broadcasted_iota(jnp.int32, st.shape, 1) + q0
                st = jnp.where(r <= c, st, NEG_BIG)
            return st

        pending = [scores(*op) for op in ops[:ATTN_LOOKAHEAD]]
        for i, (k0, ci) in enumerate(ops):
            if i + ATTN_LOOKAHEAD < len(ops):
                pending.append(scores(*ops[i + ATTN_LOOKAHEAD]))
            st = pending.pop(0)
            m, l, acc = carry[ci]
            hh = chains[ci][0]
            vsl = slice(hh * MLA_VDIM, (hh + 1) * MLA_VDIM)
            m_new = jnp.maximum(m, jnp.max(st, axis=0, keepdims=True))
            alpha = jnp.exp2(m - m_new)
            pt = jnp.exp2(st - m_new)
            l = alpha * l + jnp.sum(pt, axis=0, keepdims=True)
            acc = alpha * acc + _dot(vt_ref[j, vsl, k0:k0 + ATTN_KEY_TILE], pt.astype(BF16))
            carry[ci] = (m_new, l, acc)
        return tuple(carry)

    qw = ATTN_QUERY_TILE
    init = tuple((jnp.full((1, qw), NEG_BIG, F32), jnp.zeros((1, qw), F32), jnp.zeros((MLA_VDIM, qw), F32))
                 for _ in chains)
    carry = lax.fori_loop(0, qi, functools.partial(block, masked=False), init)
    carry = block(qi, carry, True)
    outs = [acc / l for _, l, acc in carry]
    heads = [jnp.concatenate(outs[hh * qh_n:(hh + 1) * qh_n], axis=1) for hh in range(HEADS_PER_STEP)]
    o_ref[...] = jnp.concatenate(heads, axis=0).T.astype(o_ref.dtype)


def _mla_prompt(qt, k, vt, b, t, bq):
    hp = HEADS_PER_STEP
    nq = t // bq
    return pl.pallas_call(
        functools.partial(_mla_prompt_kernel, bq=bq),
        grid=(b, MLA_HEADS // hp, nq),
        in_specs=[pl.BlockSpec((None, hp * HEAD_PAD, bq), lambda bi, hi, qi: (bi * nq + qi, hi, 0)),
                  pl.BlockSpec((None, t, hp * HEAD_PAD), lambda bi, hi, qi: (bi, 0, hi)),
                  pl.BlockSpec((nq, hp * MLA_VDIM, bq), lambda bi, hi, qi: (bi, hi, 0))],
        out_specs=pl.BlockSpec((None, bq, hp * MLA_VDIM), lambda bi, hi, qi: (bi, qi, hi)),
        out_shape=jax.ShapeDtypeStruct((b, t, MLA_WIDTH), BF16),
        compiler_params=_params("parallel", "parallel", "arbitrary"),
        name="mla_prompt",
    )(qt, k, vt)


def _lower_bound(raw):
    e = jnp.exp(raw - jnp.max(raw, axis=0, keepdims=True))
    return e[0:1] / jnp.sum(e, axis=0, keepdims=True)


def _hgrn_prompt_kernel(q_ref, f_ref, i_ref, gate_ref, lbraw_ref, gn_ref, o_ref, s_ref,
                        qt_scr, o_scr, b_scr, u_scr, *, t):
    c = HG_CHUNK
    g = HG_GROUP
    cpg = g // c
    lb = _lower_bound(lbraw_ref[...])
    pos = lax.broadcasted_iota(jnp.int32, (g, HG_DK), 0) % c
    r2 = lax.broadcasted_iota(jnp.int32, (g, g), 0)
    c2 = lax.broadcasted_iota(jnp.int32, (g, g), 1)
    amask = jnp.logical_and(c2 <= r2, c2 >= r2 - r2 % c)

    def group(gi, carry):
        rows = pl.ds(pl.multiple_of(gi * g, g), g)
        q = q_ref[rows, :]
        vb = i_ref[rows, :].astype(BF16)
        f = lb + (1.0 - lb) * jax.nn.sigmoid(f_ref[rows, :])
        k = 1.0 - f
        bcum = jnp.log(f)
        for sh in (1, 2, 4, 8, 16):
            bcum = bcum + jnp.where(pos >= sh, pltpu.roll(bcum, sh, axis=0), 0.0)
        b_scr[rows, :] = bcum
        q_t = (q * jnp.exp(bcum)).astype(BF16)
        k_t = (k * jnp.exp(-bcum)).astype(BF16)
        qt_scr[rows, :] = q_t
        a = _dot_nt(q_t, k_t)
        b3 = bcum.reshape(cpg, c, HG_DK)
        k_s = (k.reshape(cpg, c, HG_DK) * jnp.exp(b3[:, c - 1:c, :] - b3)).astype(BF16)
        v3 = vb.reshape(cpg, c, HG_DV)
        for ci in range(cpg):
            u_scr[gi * cpg + ci] = _dot_tn(v3[ci], k_s[ci])
        o_scr[rows, :] = _dot(jnp.where(amask, a, 0.0).astype(BF16), vb)
        return carry

    lax.fori_loop(0, t // g, group, 0, unroll=2)

    def step(ci, st):
        rows = pl.ds(pl.multiple_of(ci * c, c), c)
        o_scr[rows, :] = o_scr[rows, :] + _dot_nt(qt_scr[rows, :], st.astype(BF16))
        decay = jnp.exp(b_scr[pl.ds(ci * c + c - 1, 1), :])
        return st * decay + u_scr[ci]

    st = lax.fori_loop(0, t // c, step, jnp.zeros((HG_DV, HG_DK), F32), unroll=4)
    s_ref[...] = st.T

    gn = gn_ref[...]

    def finish(gi, carry):
        rows = pl.ds(pl.multiple_of(gi * g, g), g)
        o_ref[rows, :] = (_rms(o_scr[rows, :], gn) * _silu(gate_ref[rows, :])).astype(o_ref.dtype)
        return carry

    lax.fori_loop(0, t // g, finish, 0)


def _hgrn_prompt(zh, lb_raw, gn):
    b, t, _ = zh.shape
    col = lambda off: pl.BlockSpec((None, t, HG_DK), lambda bi, hi: (bi, 0, off + hi))
    return pl.pallas_call(
        functools.partial(_hgrn_prompt_kernel, t=t),
        grid=(b, HG_HEADS),
        in_specs=[col(0), col(HG_HEADS), col(2 * HG_HEADS), col(3 * HG_HEADS),
                  pl.BlockSpec((lb_raw.shape[0], HG_DK), lambda bi, hi: (0, hi)), _const_spec((1, HG_DV))],
        out_specs=[pl.BlockSpec((None, t, HG_DV), lambda bi, hi: (bi, 0, hi)),
                   pl.BlockSpec((None, None, HG_DK, HG_DV), lambda bi, hi: (bi, hi, 0, 0))],
        out_shape=[jax.ShapeDtypeStruct((b, t, HG_WIDTH), BF16),
                   jax.ShapeDtypeStruct((b, HG_HEADS, HG_DK, HG_DV), F32)],
        scratch_shapes=[pltpu.VMEM((t, HG_DK), BF16), pltpu.VMEM((t, HG_DV), F32), pltpu.VMEM((t, HG_DK), F32),
                        pltpu.VMEM((t // HG_CHUNK, HG_DV, HG_DK), F32)],
        compiler_params=_params("parallel", "parallel"),
        name="hgrn_prompt",
    )(zh, zh, zh, zh, lb_raw, gn)


def _qlat_kernel(q_ref, wukt_ref, o_ref):
    for hd in range(MLA_HEADS):
        o_ref[hd] = _dot(q_ref[:, hd * HEAD_PAD:(hd + 1) * HEAD_PAD], wukt_ref[hd])


def _qlat(q_pad, wukt):
    n = q_pad.shape[0]
    return pl.pallas_call(
        _qlat_kernel,
        out_shape=jax.ShapeDtypeStruct((MLA_HEADS, n, KV_LORA), F32),
        compiler_params=pltpu.CompilerParams(vmem_limit_bytes=VMEM_LIMIT),
        name="qlat",
    )(q_pad, wukt)


def _paged_kernel(pt_ref, qlat_ref, qrope_ref, ckvn_ref, krn_ref, ckv_hbm, krt_hbm, lat_ref,
                  kvbuf, krbuf, kvb, s_scr, sem, *, n_pages):
    b = pl.program_id(0)
    nb = pl.num_programs(0)
    pc = PAGES_PER_CHUNK
    pb = PAGES_PER_BLOCK
    ns = PAGED_SLOTS
    ahead = ns - 1
    n_chunks = n_pages // pc
    page = kvbuf.shape[2]

    def copies(bb, ch, slot):
        out = []
        for p in range(pc):
            pg = pt_ref[bb, ch * pc + p]
            out.append(pltpu.make_async_copy(ckv_hbm.at[pg], kvbuf.at[slot, p], sem.at[0, slot]))
            out.append(pltpu.make_async_copy(krt_hbm.at[pg], krbuf.at[slot, p], sem.at[1, slot]))
        return out

    def start(bb, ch, slot):
        for cp in copies(bb, ch, slot):
            cp.start()

    def locate(ch_in_b):
        over = ch_in_b // n_chunks
        return jnp.minimum(b + over, nb - 1), ch_in_b - over * n_chunks

    def prefetch(ch_in_b):
        tb, tch = locate(ch_in_b + ahead)
        start(tb, tch, (ch_in_b + ahead) % ns)

    def load_and_score(ch_in_b):
        tb, tch = locate(ch_in_b)
        slot = ch_in_b % ns
        for cp in copies(tb, tch, slot):
            cp.wait()
        par = ch_in_b % 2
        qlat_b = qlat_ref[tb].astype(BF16)
        qrope_b = qrope_ref[tb].astype(BF16)
        scores = []
        for j in range(pc // pb):
            kv = kvbuf[slot, pl.ds(j * pb, pb)].reshape(pb * page, KV_LORA).astype(BF16)
            kvb[par, pl.ds(j * pb * page, pb * page), :] = kv
            krt = jnp.concatenate([krbuf[slot, j * pb + p] for p in range(pb)], axis=1).astype(BF16)
            scores.append(_dot_nt(qlat_b, kv) + _dot(qrope_b, krt))
        return jnp.concatenate(scores, axis=1)

    @pl.when(b == 0)
    def _():
        for ch0 in range(ahead):
            start(0, ch0, ch0)
        prefetch(0)
        s_scr[...] = load_and_score(0)

    def chunk(ch, carry):
        s, m, l, acc = carry
        prefetch(ch + 1)
        s_next = load_and_score(ch + 1)
        m_new = jnp.maximum(m, jnp.max(s, axis=-1, keepdims=True))
        alpha = jnp.exp(m - m_new)
        p = jnp.exp(s - m_new)
        l = alpha * l + jnp.sum(p, axis=-1, keepdims=True)
        acc = alpha * acc + _dot(p.astype(BF16), kvb[ch % 2])
        return s_next, m_new, l, acc

    init = (s_scr[...], jnp.full((MLA_HEADS, 1), NEG_BIG, F32), jnp.zeros((MLA_HEADS, 1), F32),
            jnp.zeros((MLA_HEADS, KV_LORA), F32))
    s, m, l, acc = lax.fori_loop(0, n_chunks, chunk, init)
    s_scr[...] = s

    @pl.when(b == nb - 1)
    def _():
        for i in range(1, ahead + 1):
            tb, tch = locate(n_chunks + i)
            for cp in copies(tb, tch, (n_chunks + i) % ns):
                cp.wait()

    qlat = qlat_ref[b]
    qrope = qrope_ref[b]
    ckvn = ckvn_ref[...]
    s_new = (jnp.sum(qlat * ckvn, axis=-1, keepdims=True)
             + jnp.sum(qrope * krn_ref[...], axis=-1, keepdims=True))
    m_new = jnp.maximum(m, s_new)
    alpha = jnp.exp(m - m_new)
    p_new = jnp.exp(s_new - m_new)
    l = alpha * l + p_new
    lat_ref[...] = (alpha * acc + p_new * ckvn) / l


def _paged_attention(page_table, qlat, qrope, ckv_new, kr_new, cache_ckv, cache_kr):
    n, n_pages = page_table.shape
    page = cache_ckv.shape[1]
    pc = PAGES_PER_CHUNK
    per = lambda shape: pl.BlockSpec((None,) + shape, lambda bi, pt: (bi, 0, 0))
    whole = lambda shape: pl.BlockSpec(shape, lambda bi, pt: (0, 0, 0))
    grid_spec = pltpu.PrefetchScalarGridSpec(
        num_scalar_prefetch=1,
        grid=(n,),
        in_specs=[whole((n, MLA_HEADS, KV_LORA)), whole((n, MLA_HEADS, MLA_ROPE)), per((1, KV_LORA)),
                  per((1, MLA_ROPE)), pl.BlockSpec(memory_space=pl.ANY), pl.BlockSpec(memory_space=pl.ANY)],
        out_specs=per((MLA_HEADS, KV_LORA)),
        scratch_shapes=[pltpu.VMEM((PAGED_SLOTS, pc, page, KV_LORA), F32),
                        pltpu.VMEM((PAGED_SLOTS, pc, MLA_ROPE, page), F32),
                        pltpu.VMEM((2, pc * page, KV_LORA), BF16),
                        pltpu.VMEM((MLA_HEADS, pc * page), F32),
                        pltpu.SemaphoreType.DMA((2, PAGED_SLOTS))],
    )
    return pl.pallas_call(
        functools.partial(_paged_kernel, n_pages=n_pages),
        grid_spec=grid_spec,
        out_shape=jax.ShapeDtypeStruct((n, MLA_HEADS, KV_LORA), F32),
        compiler_params=_params("arbitrary"),
        name="paged_mla",
    )(page_table, qlat, qrope, ckv_new, kr_new, cache_ckv, cache_kr)


def _latv_kernel(lat_ref, wuv_ref, o_ref):
    outs = [_dot(lat_ref[hd].astype(BF16), wuv_ref[:, hd * MLA_VDIM:(hd + 1) * MLA_VDIM])
            for hd in range(MLA_HEADS)]
    o_ref[...] = jnp.concatenate(outs, axis=-1).astype(o_ref.dtype)


def _latv(lat_hb, wuv):
    n = lat_hb.shape[1]
    return pl.pallas_call(
        _latv_kernel,
        out_shape=jax.ShapeDtypeStruct((n, MLA_WIDTH), BF16),
        compiler_params=pltpu.CompilerParams(vmem_limit_bytes=VMEM_LIMIT),
        name="latv",
    )(lat_hb, wuv)


HG_SAMPLE_BLOCK = 8


def _hgrn_sample_kernel(s_ref, zh_ref, lbraw_ref, gn_ref, so_ref, o_ref):
    gn = gn_ref[...]
    nbk = HG_SAMPLE_BLOCK
    pad = jnp.zeros((HG_DK - nbk, HG_DK), F32)
    heads = []
    for hd in range(HG_HEADS):
        lb = _lower_bound(lbraw_ref[:, hd * HG_DK:(hd + 1) * HG_DK])
        f_rows = lb + (1.0 - lb) * jax.nn.sigmoid(zh_ref[:, HG_QK + hd * HG_DK:HG_QK + (hd + 1) * HG_DK])
        f_cols = jnp.concatenate([f_rows, pad], axis=0).T
        q_cols = jnp.concatenate([zh_ref[:, hd * HG_DK:(hd + 1) * HG_DK], pad], axis=0).T
        outs = []
        for j in range(nbk):
            f = f_cols[:, j:j + 1]
            v = zh_ref[j:j + 1, 2 * HG_QK + hd * HG_DV:2 * HG_QK + (hd + 1) * HG_DV]
            s_new = f * s_ref[j, hd] + (1.0 - f) * v
            so_ref[j, hd] = s_new
            o = jnp.sum(q_cols[:, j:j + 1] * s_new, axis=0, keepdims=True)
            outs.append(_rms(o, gn))
        heads.append(jnp.concatenate(outs, axis=0))
    o = jnp.concatenate(heads, axis=-1)
    gate = zh_ref[:, 2 * HG_QK + HG_WIDTH:]
    o_ref[...] = (o * _silu(gate)).astype(o_ref.dtype)


def _hgrn_sample(state, zh, lb_raw, gn):
    n = state.shape[0]
    nb = HG_SAMPLE_BLOCK
    blk4 = pl.BlockSpec((nb, HG_HEADS, HG_DK, HG_DV), lambda i: (i, 0, 0, 0))
    return pl.pallas_call(
        _hgrn_sample_kernel,
        grid=(n // nb,),
        in_specs=[blk4, pl.BlockSpec((nb, zh.shape[1]), lambda i: (i, 0)),
                  _const_spec(lb_raw.shape), _const_spec((1, HG_DV))],
        out_specs=[blk4, pl.BlockSpec((nb, HG_WIDTH), lambda i: (i, 0))],
        out_shape=[jax.ShapeDtypeStruct(state.shape, F32), jax.ShapeDtypeStruct((n, HG_WIDTH), BF16)],
        compiler_params=_params("parallel"),
        name="hgrn_sample",
    )(state, zh, lb_raw, gn)


def _outproj_kernel(x_ref, omla_ref, ohg_ref, wout_ref, g_ref, wmq_ref, x1_ref, qm_ref):
    x1 = (x_ref[...] + _dot(omla_ref[...], wout_ref[:MLA_WIDTH, :])
          + _dot(ohg_ref[...], wout_ref[MLA_WIDTH:, :]))
    x1_ref[...] = x1
    hm = _rms(x1, g_ref[...]).astype(BF16)
    qm_ref[...] = (_dot(hm, wmq_ref[...]) * (MEM_HDIM ** -0.5)).astype(qm_ref.dtype)


def _outproj(x, omla, ohg, w, tm, q_dtype):
    rows = x.shape[0]
    row = lambda wd: pl.BlockSpec((tm, wd), lambda i: (i, 0))
    return pl.pallas_call(
        _outproj_kernel,
        grid=(rows // tm,),
        in_specs=[row(D_MODEL), row(MLA_WIDTH), row(HG_WIDTH), _const_spec((MLA_WIDTH + HG_WIDTH, D_MODEL)),
                  _const_spec((1, D_MODEL)), _const_spec((D_MODEL, MEM_WIDTH))],
        out_specs=[row(D_MODEL), row(MEM_WIDTH)],
        out_shape=[jax.ShapeDtypeStruct((rows, D_MODEL), F32), jax.ShapeDtypeStruct((rows, MEM_WIDTH), q_dtype)],
        compiler_params=_params("parallel"),
        name="outproj",
    )(x, omla, ohg, w["w_out"], w["norm_memx"], w["w_mq"])


def _memattn_prompt_kernel(q_ref, k_ref, v_ref, o_ref):
    outs = []
    for hd in range(MEM_HEADS):
        sl = slice(hd * MEM_HDIM, (hd + 1) * MEM_HDIM)
        s = _dot_nt(q_ref[:, sl], k_ref[:, sl])
        p = jnp.exp(s - jnp.max(s, axis=-1, keepdims=True))
        o = _dot(p.astype(BF16), v_ref[:, sl])
        outs.append(o / jnp.sum(p, axis=-1, keepdims=True))
    o_ref[...] = jnp.concatenate(outs, axis=-1).astype(o_ref.dtype)


def _memattn_prompt(q, k, v, tm):
    b, t, _ = q.shape
    m = k.shape[1]
    return pl.pallas_call(
        _memattn_prompt_kernel,
        grid=(b, t // tm),
        in_specs=[pl.BlockSpec((None, tm, MEM_WIDTH), lambda bi, ti: (bi, ti, 0)),
                  pl.BlockSpec((None, m, MEM_WIDTH), lambda bi, ti: (bi, 0, 0)),
                  pl.BlockSpec((None, m, MEM_WIDTH), lambda bi, ti: (bi, 0, 0))],
        out_specs=pl.BlockSpec((None, tm, MEM_WIDTH), lambda bi, ti: (bi, ti, 0)),
        out_shape=jax.ShapeDtypeStruct((b, t, MEM_WIDTH), BF16),
        compiler_params=_params("parallel", "parallel"),
        name="memattn_prompt",
    )(q, k, v)


def _memattn_sample_kernel(q_ref, k_ref, v_ref, o_ref):
    nblk, rows = k_ref.shape[:2]
    hid = lax.broadcasted_iota(jnp.int32, (SUBLANE, rows), 0)
    rhd = lax.broadcasted_iota(jnp.int32, (SUBLANE, rows), 1) % MEM_HEADS
    own = hid == rhd
    zero = jnp.zeros((SUBLANE - MEM_HEADS, MEM_HDIM), F32)
    scores = []
    for j in range(nblk):
        q = jnp.concatenate([q_ref[j, :, hd * MEM_HDIM:(hd + 1) * MEM_HDIM] for hd in range(MEM_HEADS)] + [zero],
                            axis=0)
        scores.append(_dot_nt(q.astype(BF16), k_ref[j].astype(BF16)))
    for j in range(nblk):
        s = jnp.where(own, scores[j], NEG_BIG)
        p = jnp.exp(s - jnp.max(s, axis=-1, keepdims=True))
        p = jnp.where(own, p, 0.0)
        p = p / jnp.sum(p, axis=-1, keepdims=True)
        of = _dot(p.astype(BF16), v_ref[j].astype(BF16))
        o_ref[j] = jnp.concatenate([of[hd:hd + 1] for hd in range(MEM_HEADS)], axis=-1).astype(o_ref.dtype)


MEM_SAMPLE_BLOCK = 4


def _memattn_sample(q, k, v):
    n, m = k.shape[:2]
    sb = MEM_SAMPLE_BLOCK
    return pl.pallas_call(
        _memattn_sample_kernel,
        grid=(n // sb,),
        in_specs=[pl.BlockSpec((sb, 1, MEM_WIDTH), lambda i: (i, 0, 0)),
                  pl.BlockSpec((sb, m, MEM_HDIM), lambda i: (i, 0, 0)),
                  pl.BlockSpec((sb, m, MEM_HDIM), lambda i: (i, 0, 0))],
        out_specs=pl.BlockSpec((sb, 1, MEM_WIDTH), lambda i: (i, 0, 0)),
        out_shape=jax.ShapeDtypeStruct((n, 1, MEM_WIDTH), BF16),
        compiler_params=_params("parallel"),
        name="memattn_sample",
    )(q, k, v)


def _ffn_kernel(*refs, seq_mode, tiles_per_seq):
    if seq_mode:
        (x1_ref, om_ref, wmo_ref, g_ref, wup_ref, cw_ref, cb_ref, wdn_ref, gfin_ref,
         y_ref, tail_ref, carry_ref) = refs
    else:
        (x1_ref, om_ref, wmo_ref, g_ref, wup_ref, cw_ref, cb_ref, wdn_ref, gfin_ref, prev2_ref, prev1_ref,
         y_ref, tail_ref) = refs
    tm = x1_ref.shape[0]
    x2 = x1_ref[...] + _dot(om_ref[...], wmo_ref[...])
    h = _rms(x2, g_ref[...]).astype(BF16)
    acc = x2
    if seq_mode:
        first = pl.program_id(0) % tiles_per_seq == 0
        rid = lax.broadcasted_iota(jnp.int32, (tm, 1), 0)

        @pl.when(pl.program_id(0) == 0)
        def _():
            carry_ref[...] = jnp.zeros_like(carry_ref)
    for f0 in range(0, D_FF, FFN_CHUNK):
        fs = slice(f0, f0 + FFN_CHUNK)
        a = _dot(h, wup_ref[:, fs])
        gt = _dot(h, wup_ref[:, D_FF + f0:D_FF + f0 + FFN_CHUNK])
        if seq_mode:
            prev = jnp.where(first, 0.0, carry_ref[:, fs])
            p1 = prev[SUBLANE - 1:SUBLANE]
            p2 = prev[SUBLANE - 2:SUBLANE - 1]
            a1 = jnp.where(rid == 0, p1, pltpu.roll(a, 1, axis=0))
            a2 = jnp.where(rid == 0, p2, jnp.where(rid == 1, p1, pltpu.roll(a, 2, axis=0)))
            carry_ref[:, fs] = a[tm - SUBLANE:, :]
            tail_ref[:, fs] = a[tm - (CONV_W - 1):, :]
        else:
            a1 = prev1_ref[:, fs]
            a2 = prev2_ref[:, fs]
            tail_ref[:, fs] = a
        conv = cb_ref[:, fs] + a2 * cw_ref[0:1, fs] + a1 * cw_ref[1:2, fs] + a * cw_ref[2:3, fs]
        u = (_silu(conv) * gt).astype(BF16)
        acc = acc + _dot(u, wdn_ref[fs, :])
    y_ref[...] = _rms(acc, gfin_ref[...])


def _ffn(x1, om, w, tm, seq_len=None, prev=None):
    rows = x1.shape[0]
    seq_mode = prev is None
    row = lambda wd: pl.BlockSpec((tm, wd), lambda i: (i, 0))
    in_specs = [row(D_MODEL), row(MEM_WIDTH), _const_spec((MEM_WIDTH, D_MODEL)), _const_spec((1, D_MODEL)),
                _const_spec((D_MODEL, 2 * D_FF)), _const_spec((CONV_W, D_FF)), _const_spec((1, D_FF)),
                _const_spec((D_FF, D_MODEL)), _const_spec((1, D_MODEL))]
    args = [x1, om, w["w_mo"], w["norm_ffn"], w["w_up"], w["conv_w"], w["conv_b"], w["w_down"], w["norm_final"]]
    if seq_mode:
        tiles_per_seq = seq_len // tm
        n_seq = rows // seq_len
        tail_spec = pl.BlockSpec((None, CONV_W - 1, D_FF), lambda i: (i // tiles_per_seq, 0, 0))
        tail_shape = jax.ShapeDtypeStruct((n_seq, CONV_W - 1, D_FF), F32)
        scratch = [pltpu.VMEM((SUBLANE, D_FF), F32)]
        sem = "arbitrary"
    else:
        tiles_per_seq = 1
        in_specs += [row(D_FF), row(D_FF)]
        args += list(prev)
        tail_spec = row(D_FF)
        tail_shape = jax.ShapeDtypeStruct((rows, D_FF), F32)
        scratch = []
        sem = "parallel"
    return pl.pallas_call(
        functools.partial(_ffn_kernel, seq_mode=seq_mode, tiles_per_seq=tiles_per_seq),
        grid=(rows // tm,),
        in_specs=in_specs,
        out_specs=[row(D_MODEL), tail_spec],
        out_shape=[jax.ShapeDtypeStruct((rows, D_MODEL), F32), tail_shape],
        scratch_shapes=scratch,
        compiler_params=_params(sem),
        name="ffn_seq" if seq_mode else "ffn_step",
    )(*args)


def _rope_tables(pos):
    half = ROPE_HALF
    inv = ROPE_BASE ** (-jnp.arange(half, dtype=F32) / half)
    ang = pos[:, None] * inv[None, :]
    cos, sin = jnp.cos(ang), jnp.sin(ang)
    n = pos.shape[0]
    one = jnp.ones((n, MLA_NOPE), F32)
    zero = jnp.zeros((n, MLA_NOPE), F32)
    pad = jnp.zeros((n, LANE - MLA_NOPE - MLA_ROPE), F32)
    cos_t = jnp.concatenate([one, cos, cos, pad], axis=-1)
    sin_t = jnp.concatenate([zero, -sin, sin, pad], axis=-1)
    return cos_t, sin_t


def _rope_tile(w_rope):
    x1, x2 = w_rope[..., :ROPE_HALF], w_rope[..., ROPE_HALF:]
    return jnp.concatenate([x1, x2, x2, x1], axis=-1)


def _prep_weights(norm_mix, w_in, norm_q, norm_kv, w_uq, w_uk, w_uv, norm_hg, w_out, norm_memx, norm_mem,
                  w_mq, w_mk, w_mv, w_mo, norm_ffn, w_up, conv_w, conv_b, w_down, norm_final):
    off_kr = Q_LORA + KV_LORA
    off_hg = off_kr + MLA_ROPE
    kr_tile = jnp.concatenate([jnp.zeros((D_MODEL, MLA_NOPE), F32), _rope_tile(w_in[:, off_kr:off_hg])], axis=-1)
    w_in_pad = jnp.concatenate([w_in[:, :off_kr], kr_tile, w_in[:, off_hg:]], axis=-1)
    uq = jnp.concatenate([w_uq[..., :MLA_NOPE], _rope_tile(w_uq[..., MLA_NOPE:])], axis=-1)
    uk = jnp.concatenate([w_uk, jnp.zeros((KV_LORA, MLA_HEADS, HEAD_PAD - MLA_NOPE), F32)], axis=-1)
    ukt = jnp.transpose(uk, (1, 2, 0))
    row = lambda g: g.reshape(1, -1)
    return {
        "norm_mix": row(norm_mix), "w_in": w_in_pad.astype(BF16), "norm_q": row(norm_q), "norm_kv": row(norm_kv),
        "w_uq": uq.reshape(Q_LORA, MLA_HEADS * HEAD_PAD).astype(BF16),
        "w_uq_t": uq.reshape(Q_LORA, MLA_HEADS * HEAD_PAD).T.astype(BF16),
        "w_uv_t": w_uv.reshape(KV_LORA, MLA_WIDTH).T.astype(BF16),
        "w_uk": uk.reshape(KV_LORA, MLA_HEADS * HEAD_PAD).astype(BF16),
        "w_ukt": ukt.astype(BF16),
        "w_uv": w_uv.reshape(KV_LORA, MLA_WIDTH).astype(BF16),
        "norm_hg": row(norm_hg), "w_out": w_out.astype(BF16), "norm_memx": row(norm_memx),
        "norm_mem": row(norm_mem), "w_mq": w_mq.reshape(D_MODEL, MEM_WIDTH).astype(BF16),
        "w_mk": w_mk.reshape(D_MODEL, MEM_WIDTH).astype(BF16), "w_mv": w_mv.reshape(D_MODEL, MEM_WIDTH).astype(BF16),
        "w_mo": w_mo.reshape(MEM_WIDTH, D_MODEL).astype(BF16), "norm_ffn": row(norm_ffn),
        "w_up": w_up.astype(BF16), "conv_w": conv_w, "conv_b": row(conv_b), "w_down": w_down.astype(BF16),
        "norm_final": row(norm_final),
    }


def kernel(x_prompt, x_sample, mem_prompt, cache_ckv, cache_krope, page_table, cache_mem_k, cache_mem_v,
           state_hgrn, state_conv, norm_mix, w_in, norm_q, norm_kv, w_uq, w_uk, w_uv, hg_lb_raw, norm_hg,
           w_out, norm_memx, norm_mem, w_mq, w_mk, w_mv, w_mo, norm_ffn, w_up, conv_w, conv_b, w_down,
           norm_final):
    bp, t, _ = x_prompt.shape
    db = x_sample.shape[0]
    n_pages = page_table.shape[1]
    page = cache_ckv.shape[2]
    past_len = n_pages * page
    w = _prep_weights(norm_mix[0], w_in[0], norm_q[0], norm_kv[0], w_uq[0], w_uk[0], w_uv[0], norm_hg[0],
                      w_out[0], norm_memx[0], norm_mem[0], w_mq[0], w_mk[0], w_mv[0], w_mo[0], norm_ffn[0],
                      w_up[0], conv_w[0], conv_b[0], w_down[0], norm_final)
    tm = ROW_TILE
    rows_p = bp * t

    mk_p, mv_p, mk_pb, mv_pb = _memkv(mem_prompt.reshape(-1, D_MODEL), w["norm_mem"], w["w_mk"], w["w_mv"])
    cos_p, sin_p = _rope_tables(jnp.arange(t, dtype=F32))
    xp = x_prompt.reshape(rows_p, D_MODEL)
    qt_p, k_p, vt_p, ckv_p, kr_p, zh_p = _mix_in(xp, w, cos_p, sin_p, tm, True)
    hq = MLA_HEADS * HEAD_PAD
    o_mla_p = _mla_prompt(qt_p, k_p.reshape(bp, t, hq), vt_p, bp, t, tm)
    o_hg_p, s_p = _hgrn_prompt(zh_p.reshape(bp, t, -1), hg_lb_raw, w["norm_hg"])
    x1_p, qm_p = _outproj(xp, o_mla_p.reshape(rows_p, -1), o_hg_p.reshape(rows_p, -1), w, tm, BF16)
    m_tok = mem_prompt.shape[1]
    om_p = _memattn_prompt(qm_p.reshape(bp, t, MEM_WIDTH), mk_pb.reshape(bp, m_tok, MEM_WIDTH),
                           mv_pb.reshape(bp, m_tok, MEM_WIDTH), tm)
    y_p, cv_p = _ffn(x1_p, om_p.reshape(rows_p, MEM_WIDTH), w, tm, seq_len=t)

    cos_s, sin_s = _rope_tables(jnp.full((db,), past_len, F32))
    xs = x_sample.reshape(db, D_MODEL)
    q_s, _, _, ckv_s, kr_s, zh_s = _mix_in(xs, w, cos_s, sin_s, db, False)
    qlat = jnp.transpose(_qlat(q_s, w["w_ukt"]), (1, 0, 2))
    qrope = q_s.reshape(db, MLA_HEADS, HEAD_PAD)[:, :, MLA_NOPE:MLA_NOPE + MLA_ROPE].astype(F32)
    kr_new = kr_s[:, MLA_NOPE:MLA_NOPE + MLA_ROPE]
    lat = _paged_attention(page_table, qlat, qrope, ckv_s.reshape(db, 1, KV_LORA), kr_new.reshape(db, 1, MLA_ROPE),
                           cache_ckv.reshape(-1, page, KV_LORA),
                           jnp.swapaxes(cache_krope, 2, 3).reshape(-1, MLA_ROPE, page))
    o_mla_s = _latv(jnp.transpose(lat, (1, 0, 2)), w["w_uv"])
    s_s, o_hg_s = _hgrn_sample(state_hgrn.reshape(db, HG_HEADS, HG_DK, HG_DV), zh_s, hg_lb_raw, w["norm_hg"])
    x1_s, qm_s = _outproj(xs, o_mla_s, o_hg_s, w, db, F32)
    om_s = _memattn_sample(qm_s.reshape(db, 1, MEM_WIDTH), cache_mem_k.reshape(db, m_tok * MEM_HEADS, MEM_HDIM),
                           cache_mem_v.reshape(db, m_tok * MEM_HEADS, MEM_HDIM))
    y_s, a_s = _ffn(x1_s, om_s.reshape(db, MEM_WIDTH), w, db, prev=(state_conv[0, :, 0], state_conv[0, :, 1]))
    cv_s = jnp.stack([state_conv[0, :, 1], a_s], axis=1)

    rope_sl = slice(MLA_NOPE, MLA_NOPE + MLA_ROPE)
    return (y_p.reshape(bp, t, D_MODEL), y_s.reshape(db, 1, D_MODEL),
            ckv_p.reshape(1, bp, t, KV_LORA), kr_p[:, rope_sl].reshape(1, bp, t, MLA_ROPE),
            mk_p.reshape(1, bp, m_tok, MEM_HEADS, MEM_HDIM), mv_p.reshape(1, bp, m_tok, MEM_HEADS, MEM_HDIM),
            s_p[None], cv_p[None],
            ckv_s.reshape(1, db, 1, KV_LORA), kr_new.reshape(1, db, 1, MLA_ROPE),
            s_s[None], cv_s[None])
```

```python
import functools

import jax
import jax.numpy as jnp
from jax import lax
from jax.experimental import pallas as pl
from jax.experimental.pallas import tpu as pltpu

F32 = jnp.float32
BF16 = jnp.bfloat16

D_MODEL = 1024
MLA_HEADS = 8
MLA_NOPE = 64
MLA_ROPE = 32
MLA_VDIM = 64
Q_LORA = 384
KV_LORA = 256
HG_HEADS = 4
HG_DK = 128
HG_DV = 128
HG_CHUNK = 32
HG_QK = HG_HEADS * HG_DK
HG_WIDTH = HG_HEADS * HG_DV
MLA_WIDTH = MLA_HEADS * MLA_VDIM
MEM_HEADS = 4
MEM_HDIM = 128
MEM_WIDTH = MEM_HEADS * MEM_HDIM
D_FF = 2816
CONV_W = 3
ROPE_BASE = 10000.0
EPS = 1e-6

LANE = 128
SUBLANE = 8
HEAD_PAD = LANE
ROPE_HALF = MLA_ROPE // 2
OFF_CKV = Q_LORA
OFF_KRT = OFF_CKV + KV_LORA
OFF_HGP = OFF_KRT + LANE
IN_COLS_PAD = OFF_HGP + 2 * HG_QK + 2 * HG_WIDTH
VMEM_LIMIT = 56 * 1024 * 1024
ROW_TILE = 512
FFN_CHUNK = D_FF // 2
PAGES_PER_CHUNK = 16
PAGES_PER_BLOCK = 8
PAGED_SLOTS = 4
HG_GROUP = 256
NEG_BIG = -1e30
LOG2_E = 1.4426950408889634


def _rms(x, g):
    return x * lax.rsqrt(jnp.mean(x * x, axis=-1, keepdims=True) + EPS) * g


def _dot(a, b):
    return jnp.dot(a, b, preferred_element_type=F32)


def _dot_nt(a, b):
    return lax.dot_general(a, b, (((1,), (1,)), ((), ())), preferred_element_type=F32)


def _dot_tn(a, b):
    return lax.dot_general(a, b, (((0,), (0,)), ((), ())), preferred_element_type=F32)


def _silu(x):
    return x * jax.nn.sigmoid(x)


def _params(*sem):
    return pltpu.CompilerParams(dimension_semantics=sem, vmem_limit_bytes=VMEM_LIMIT)


def _const_spec(shape):
    nd = len(shape)
    return pl.BlockSpec(shape, lambda *_: (0,) * nd, pipeline_mode=pl.Buffered(1))


def _memkv_kernel(mem_ref, g_ref, wk_ref, wv_ref, k_ref, v_ref, kb_ref, vb_ref):
    m = _rms(mem_ref[...], g_ref[...]).astype(BF16)
    k = _dot(m, wk_ref[...])
    v = _dot(m, wv_ref[...])
    k_ref[...] = k
    v_ref[...] = v
    kb_ref[...] = k.astype(BF16)
    vb_ref[...] = v.astype(BF16)


def _memkv(mem, g, wk, wv):
    rows = mem.shape[0]
    tm = min(ROW_TILE, rows)
    row = lambda w: pl.BlockSpec((tm, w), lambda i: (i, 0))
    return pl.pallas_call(
        _memkv_kernel,
        grid=(rows // tm,),
        in_specs=[row(D_MODEL), _const_spec((1, D_MODEL)), _const_spec((D_MODEL, MEM_WIDTH)),
                  _const_spec((D_MODEL, MEM_WIDTH))],
        out_specs=[row(MEM_WIDTH)] * 4,
        out_shape=[jax.ShapeDtypeStruct((rows, MEM_WIDTH), F32)] * 2
        + [jax.ShapeDtypeStruct((rows, MEM_WIDTH), BF16)] * 2,
        compiler_params=_params("parallel"),
        name="memkv",
    )(mem, g, wk, wv)


def _mix_in_kernel(*refs, transposed):
    if transposed:
        (x_ref, g_ref, win_ref, gq_ref, gkv_ref, wuq_ref, wuk_ref, wuv_ref, cos_ref, sin_ref, cost_ref, sint_ref,
         q_ref, k_ref, v_ref, ckv_ref, kr_ref, zh_ref) = refs
    else:
        (x_ref, g_ref, win_ref, gq_ref, gkv_ref, wuq_ref, wuk_ref, wuv_ref, cos_ref, sin_ref,
         q_ref, k_ref, v_ref, ckv_ref, kr_ref, zh_ref) = refs
    h = _rms(x_ref[...], g_ref[...]).astype(BF16)
    z = _dot(h, win_ref[...])
    c_q = _rms(z[:, :OFF_CKV], gq_ref[...]).astype(BF16)
    c_kv = _rms(z[:, OFF_CKV:OFF_KRT], gkv_ref[...])
    ckv_ref[...] = c_kv
    c_kvb = c_kv.astype(BF16)
    zh_ref[...] = z[:, OFF_HGP:]
    cos = cos_ref[...]
    sin = sin_ref[...]
    zkr = z[:, OFF_KRT:OFF_HGP]
    kr = zkr * cos + pltpu.roll(zkr, LANE - MLA_ROPE, axis=1) * sin
    kr_ref[...] = kr
    scale = (MLA_NOPE + MLA_ROPE) ** -0.5
    kn = _dot(c_kvb, wuk_ref[...])
    for hd in range(MLA_HEADS):
        sl = slice(hd * HEAD_PAD, (hd + 1) * HEAD_PAD)
        k_ref[:, sl] = (kn[:, sl] + kr).astype(BF16)
    if transposed:
        cos_q = cost_ref[...] * (scale * LOG2_E)
        sin_q = sint_ref[...] * (scale * LOG2_E)
        qt = _dot_nt(wuq_ref[...], c_q)
        for hd in range(MLA_HEADS):
            sl = slice(hd * HEAD_PAD, (hd + 1) * HEAD_PAD)
            qh = qt[sl, :]
            q_ref[sl, :] = (qh * cos_q + pltpu.roll(qh, LANE - MLA_ROPE, axis=0) * sin_q).astype(BF16)
        v_ref[...] = _dot_nt(wuv_ref[...], c_kvb).astype(BF16)
    else:
        cos_q = cos * scale
        sin_q = sin * scale
        q = _dot(c_q, wuq_ref[...])
        for hd in range(MLA_HEADS):
            sl = slice(hd * HEAD_PAD, (hd + 1) * HEAD_PAD)
            qh = q[:, sl]
            q_ref[:, sl] = (qh * cos_q + pltpu.roll(qh, LANE - MLA_ROPE, axis=1) * sin_q).astype(BF16)
        v_ref[...] = _dot(c_kvb, wuv_ref[...]).astype(BF16)


def _mix_in(x, w, cos_t, sin_t, tm, transposed):
    rows = x.shape[0]
    t_tiles = cos_t.shape[0] // tm
    row = lambda wd: pl.BlockSpec((tm, wd), lambda i: (i, 0))
    tab = pl.BlockSpec((tm, LANE), lambda i: (i % t_tiles, 0))
    hq = MLA_HEADS * HEAD_PAD
    in_specs = [row(D_MODEL), _const_spec((1, D_MODEL)), _const_spec((D_MODEL, IN_COLS_PAD)),
                _const_spec((1, Q_LORA)), _const_spec((1, KV_LORA))]
    args = [x, w["norm_mix"], w["w_in"], w["norm_q"], w["norm_kv"]]
    if transposed:
        tile_t = lambda wd: pl.BlockSpec((None, wd, tm), lambda i: (i, 0, 0))
        tab_t = pl.BlockSpec((LANE, tm), lambda i: (0, i % t_tiles))
        in_specs += [_const_spec((hq, Q_LORA)), _const_spec((KV_LORA, hq)), _const_spec((MLA_WIDTH, KV_LORA)),
                     tab, tab, tab_t, tab_t]
        args += [w["w_uq_t"], w["w_uk"], w["w_uv_t"], cos_t, sin_t, cos_t.T, sin_t.T]
        qv_specs = [tile_t(hq), row(hq), tile_t(MLA_WIDTH)]
        qv_shapes = [jax.ShapeDtypeStruct((rows // tm, hq, tm), BF16), jax.ShapeDtypeStruct((rows, hq), BF16),
                     jax.ShapeDtypeStruct((rows // tm, MLA_WIDTH, tm), BF16)]
    else:
        in_specs += [_const_spec((Q_LORA, hq)), _const_spec((KV_LORA, hq)), _const_spec((KV_LORA, MLA_WIDTH)),
                     tab, tab]
        args += [w["w_uq"], w["w_uk"], w["w_uv"], cos_t, sin_t]
        qv_specs = [row(hq), row(hq), row(MLA_WIDTH)]
        qv_shapes = [jax.ShapeDtypeStruct((rows, hq), BF16), jax.ShapeDtypeStruct((rows, hq), BF16),
                     jax.ShapeDtypeStruct((rows, MLA_WIDTH), BF16)]
    return pl.pallas_call(
        functools.partial(_mix_in_kernel, transposed=transposed),
        grid=(rows // tm,),
        in_specs=in_specs,
        out_specs=qv_specs + [row(KV_LORA), row(LANE), row(4 * HG_QK)],
        out_shape=qv_shapes + [jax.ShapeDtypeStruct((rows, KV_LORA), F32),
                               jax.ShapeDtypeStruct((rows, LANE), F32), jax.ShapeDtypeStruct((rows, 4 * HG_QK), F32)],
        compiler_params=_params("parallel"),
        name="mix_in_t" if transposed else "mix_in",
    )(*args)


HEADS_PER_STEP = 2
ATTN_KEY_TILE = 128
ATTN_QUERY_TILE = 256
ATTN_LOOKAHEAD = 8
ATTN_ONES_ROWS = 16


def _mla_prompt_kernel(qt_ref, k_ref, vt_ref, o_ref, *, bq):
    qi = pl.program_id(2)
    kt_n = bq // ATTN_KEY_TILE
    qh_n = bq // ATTN_QUERY_TILE
    chains = [(hh, qh) for hh in range(HEADS_PER_STEP) for qh in range(qh_n)]

    def block(j, carry, masked):
        carry = list(carry)
        ops = [(kt * ATTN_KEY_TILE, ci) for kt in range(kt_n) for ci in range(len(chains))
               if not (masked and kt * ATTN_KEY_TILE > chains[ci][1] * ATTN_QUERY_TILE + ATTN_QUERY_TILE - 1)]

        def scores(k0, ci):
            hh, qh = chains[ci]
            q0 = qh * ATTN_QUERY_TILE
            ksl = slice(hh * HEAD_PAD, (hh + 1) * HEAD_PAD)
            rows = pl.ds(pl.multiple_of(j * bq + k0, ATTN_KEY_TILE), ATTN_KEY_TILE)
            st = _dot(k_ref[rows, ksl], qt_ref[ksl, q0:q0 + ATTN_QUERY_TILE])
            if masked and k0 + ATTN_KEY_TILE - 1 > q0:
                r = lax.broadcasted_iota(jnp.int32, st.shape, 0) + k0
                c = lax.broadcasted_iota(jnp.int32, st.shape, 1) + q0
                st = jnp.where(r <= c, st, NEG_BIG)
            return st

        pending = [scores(*op) for op in ops[:ATTN_LOOKAHEAD]]
        for i, (k0, ci) in enumerate(ops):
            if i + ATTN_LOOKAHEAD < len(ops):
                pending.append(scores(*ops[i + ATTN_LOOKAHEAD]))
            st = pending.pop(0)
            m, acc = carry[ci]
            hh = chains[ci][0]
            vsl = slice(hh * MLA_VDIM, (hh + 1) * MLA_VDIM)
            m_new = jnp.maximum(m, jnp.max(st, axis=0, keepdims=True))
            alpha = jnp.exp2(m - m_new)
            pt = jnp.exp2(st - m_new)
            vt_ext = jnp.concatenate([vt_ref[j, vsl, k0:k0 + ATTN_KEY_TILE], ones], axis=0)
            acc = alpha * acc + _dot(vt_ext, pt.astype(BF16))
            carry[ci] = (m_new, acc)
        return tuple(carry)

    qw = ATTN_QUERY_TILE
    ones = jnp.ones((ATTN_ONES_ROWS, ATTN_KEY_TILE), BF16)
    init = tuple((jnp.full((1, qw), NEG_BIG, F32), jnp.zeros((MLA_VDIM + ATTN_ONES_ROWS, qw), F32)) for _ in chains)
    carry = lax.fori_loop(0, qi, functools.partial(block, masked=False), init)
    carry = block(qi, carry, True)
    outs = [acc[:MLA_VDIM] / acc[MLA_VDIM:MLA_VDIM + 1] for _, acc in carry]
    heads = [jnp.concatenate(outs[hh * qh_n:(hh + 1) * qh_n], axis=1) for hh in range(HEADS_PER_STEP)]
    o_ref[...] = jnp.concatenate(heads, axis=0).T.astype(o_ref.dtype)


def _mla_prompt(qt, k, vt, b, t, bq):
    hp = HEADS_PER_STEP
    nq = t // bq
    return pl.pallas_call(
        functools.partial(_mla_prompt_kernel, bq=bq),
        grid=(b, MLA_HEADS // hp, nq),
        in_specs=[pl.BlockSpec((None, hp * HEAD_PAD, bq), lambda bi, hi, qi: (bi * nq + qi, hi, 0)),
                  pl.BlockSpec((None, t, hp * HEAD_PAD), lambda bi, hi, qi: (bi, 0, hi)),
                  pl.BlockSpec((nq, hp * MLA_VDIM, bq), lambda bi, hi, qi: (bi, hi, 0))],
        out_specs=pl.BlockSpec((None, bq, hp * MLA_VDIM), lambda bi, hi, qi: (bi, qi, hi)),
        out_shape=jax.ShapeDtypeStruct((b, t, MLA_WIDTH), BF16),
        compiler_params=_params("parallel", "parallel", "arbitrary"),
        name="mla_prompt",
    )(qt, k, vt)


def _lower_bound(raw):
    e = jnp.exp(raw - jnp.max(raw, axis=0, keepdims=True))
    return e[0:1] / jnp.sum(e, axis=0, keepdims=True)


def _hgrn_prompt_kernel(q_ref, f_ref, i_ref, gate_ref, lbraw_ref, gn_ref, o_ref, s_ref,
                        qt_scr, o_scr, b_scr, u_scr, *, t):
    c = HG_CHUNK
    g = HG_GROUP
    cpg = g // c
    lb = _lower_bound(lbraw_ref[...])
    pos = lax.broadcasted_iota(jnp.int32, (g, HG_DK), 0) % c
    r2 = lax.broadcasted_iota(jnp.int32, (g, g), 0)
    c2 = lax.broadcasted_iota(jnp.int32, (g, g), 1)
    amask = jnp.logical_and(c2 <= r2, c2 >= r2 - r2 % c)

    def group(gi, carry):
        rows = pl.ds(pl.multiple_of(gi * g, g), g)
        q = q_ref[rows, :]
        vb = i_ref[rows, :].astype(BF16)
        f = lb + (1.0 - lb) * jax.nn.sigmoid(f_ref[rows, :])
        k = 1.0 - f
        bcum = jnp.log(f)
        for sh in (1, 2, 4, 8, 16):
            bcum = bcum + jnp.where(pos >= sh, pltpu.roll(bcum, sh, axis=0), 0.0)
        b_scr[rows, :] = bcum
        q_t = (q * jnp.exp(bcum)).astype(BF16)
        k_t = (k * jnp.exp(-bcum)).astype(BF16)
        qt_scr[rows, :] = q_t
        a = _dot_nt(q_t, k_t)
        b3 = bcum.reshape(cpg, c, HG_DK)
        k_s = (k.reshape(cpg, c, HG_DK) * jnp.exp(b3[:, c - 1:c, :] - b3)).astype(BF16)
        v3 = vb.reshape(cpg, c, HG_DV)
        for ci in range(cpg):
            u_scr[gi * cpg + ci] = _dot_tn(v3[ci], k_s[ci])
        o_scr[rows, :] = _dot(jnp.where(amask, a, 0.0).astype(BF16), vb)
        return carry

    lax.fori_loop(0, t // g, group, 0, unroll=4)

    def step(ci, st):
        rows = pl.ds(pl.multiple_of(ci * c, c), c)
        o_scr[rows, :] = o_scr[rows, :] + _dot_nt(qt_scr[rows, :], st.astype(BF16))
        decay = jnp.exp(b_scr[pl.ds(ci * c + c - 1, 1), :])
        return st * decay + u_scr[ci]

    st = lax.fori_loop(0, t // c, step, jnp.zeros((HG_DV, HG_DK), F32), unroll=16)
    s_ref[...] = st.T

    gn = gn_ref[...]

    def finish(gi, carry):
        rows = pl.ds(pl.multiple_of(gi * g, g), g)
        o_ref[rows, :] = (_rms(o_scr[rows, :], gn) * _silu(gate_ref[rows, :])).astype(o_ref.dtype)
        return carry

    lax.fori_loop(0, t // g, finish, 0, unroll=2)


def _hgrn_prompt(zh, lb_raw, gn):
    b, t, _ = zh.shape
    col = lambda off: pl.BlockSpec((None, t, HG_DK), lambda bi, hi: (bi, 0, off + hi))
    return pl.pallas_call(
        functools.partial(_hgrn_prompt_kernel, t=t),
        grid=(b, HG_HEADS),
        in_specs=[col(0), col(HG_HEADS), col(2 * HG_HEADS), col(3 * HG_HEADS),
                  pl.BlockSpec((lb_raw.shape[0], HG_DK), lambda bi, hi: (0, hi)), _const_spec((1, HG_DV))],
        out_specs=[pl.BlockSpec((None, t, HG_DV), lambda bi, hi: (bi, 0, hi)),
                   pl.BlockSpec((None, None, HG_DK, HG_DV), lambda bi, hi: (bi, hi, 0, 0))],
        out_shape=[jax.ShapeDtypeStruct((b, t, HG_WIDTH), BF16),
                   jax.ShapeDtypeStruct((b, HG_HEADS, HG_DK, HG_DV), F32)],
        scratch_shapes=[pltpu.VMEM((t, HG_DK), BF16), pltpu.VMEM((t, HG_DV), F32), pltpu.VMEM((t, HG_DK), F32),
                        pltpu.VMEM((t // HG_CHUNK, HG_DV, HG_DK), F32)],
        compiler_params=_params("parallel", "parallel"),
        name="hgrn_prompt",
    )(zh, zh, zh, zh, lb_raw, gn)


def _qlat_kernel(q_ref, wukt_ref, o_ref):
    for hd in range(MLA_HEADS):
        o_ref[hd] = _dot(q_ref[:, hd * HEAD_PAD:(hd + 1) * HEAD_PAD], wukt_ref[hd])


def _qlat(q_pad, wukt):
    n = q_pad.shape[0]
    return pl.pallas_call(
        _qlat_kernel,
        out_shape=jax.ShapeDtypeStruct((MLA_HEADS, n, KV_LORA), F32),
        compiler_params=pltpu.CompilerParams(vmem_limit_bytes=VMEM_LIMIT),
        name="qlat",
    )(q_pad, wukt)


def _paged_kernel(pt_ref, qlat_ref, qrope_ref, ckvn_ref, krn_ref, ckv_hbm, krt_hbm, lat_ref,
                  kvbuf, krbuf, kvb, s_scr, sem, *, n_pages):
    b = pl.program_id(0)
    nb = pl.num_programs(0)
    pc = PAGES_PER_CHUNK
    pb = PAGES_PER_BLOCK
    ns = PAGED_SLOTS
    ahead = ns - 1
    n_chunks = n_pages // pc
    page = kvbuf.shape[2]

    def copies(bb, ch, slot):
        out = []
        for p in range(pc):
            pg = pt_ref[bb, ch * pc + p]
            out.append(pltpu.make_async_copy(ckv_hbm.at[pg], kvbuf.at[slot, p], sem.at[0, slot]))
            out.append(pltpu.make_async_copy(krt_hbm.at[pg], krbuf.at[slot, p], sem.at[1, slot]))
        return out

    def start(bb, ch, slot):
        for cp in copies(bb, ch, slot):
            cp.start()

    def locate(ch_in_b):
        over = ch_in_b // n_chunks
        return jnp.minimum(b + over, nb - 1), ch_in_b - over * n_chunks

    def prefetch(ch_in_b):
        tb, tch = locate(ch_in_b + ahead)
        start(tb, tch, (ch_in_b + ahead) % ns)

    def load_and_score(ch_in_b):
        tb, tch = locate(ch_in_b)
        slot = ch_in_b % ns
        for cp in copies(tb, tch, slot):
            cp.wait()
        par = ch_in_b % 2
        qlat_b = qlat_ref[tb].astype(BF16)
        qrope_b = qrope_ref[tb].astype(BF16)
        scores = []
        for j in range(pc // pb):
            kv = kvbuf[slot, pl.ds(j * pb, pb)].reshape(pb * page, KV_LORA).astype(BF16)
            kvb[par, pl.ds(j * pb * page, pb * page), :] = kv
            krt = jnp.concatenate([krbuf[slot, j * pb + p] for p in range(pb)], axis=1).astype(BF16)
            scores.append(_dot_nt(qlat_b, kv) + _dot(qrope_b, krt))
        return jnp.concatenate(scores, axis=1)

    @pl.when(b == 0)
    def _():
        for ch0 in range(ahead):
            start(0, ch0, ch0)
        prefetch(0)
        s_scr[...] = load_and_score(0)

    def chunk(ch, carry):
        s, m, l, acc = carry
        prefetch(ch + 1)
        s_next = load_and_score(ch + 1)
        m_new = jnp.maximum(m, jnp.max(s, axis=-1, keepdims=True))
        alpha = jnp.exp(m - m_new)
        p = jnp.exp(s - m_new)
        l = alpha * l + jnp.sum(p, axis=-1, keepdims=True)
        acc = alpha * acc + _dot(p.astype(BF16), kvb[ch % 2])
        return s_next, m_new, l, acc

    init = (s_scr[...], jnp.full((MLA_HEADS, 1), NEG_BIG, F32), jnp.zeros((MLA_HEADS, 1), F32),
            jnp.zeros((MLA_HEADS, KV_LORA), F32))
    s, m, l, acc = lax.fori_loop(0, n_chunks, chunk, init)
    s_scr[...] = s

    @pl.when(b == nb - 1)
    def _():
        for i in range(1, ahead + 1):
            tb, tch = locate(n_chunks + i)
            for cp in copies(tb, tch, (n_chunks + i) % ns):
                cp.wait()

    qlat = qlat_ref[b]
    qrope = qrope_ref[b]
    ckvn = ckvn_ref[...]
    s_new = (jnp.sum(qlat * ckvn, axis=-1, keepdims=True)
             + jnp.sum(qrope * krn_ref[...], axis=-1, keepdims=True))
    m_new = jnp.maximum(m, s_new)
    alpha = jnp.exp(m - m_new)
    p_new = jnp.exp(s_new - m_new)
    l = alpha * l + p_new
    lat_ref[...] = (alpha * acc + p_new * ckvn) / l


def _paged_attention(page_table, qlat, qrope, ckv_new, kr_new, cache_ckv, cache_kr):
    n, n_pages = page_table.shape
    page = cache_ckv.shape[1]
    pc = PAGES_PER_CHUNK
    per = lambda shape: pl.BlockSpec((None,) + shape, lambda bi, pt: (bi, 0, 0))
    whole = lambda shape: pl.BlockSpec(shape, lambda bi, pt: (0, 0, 0))
    grid_spec = pltpu.PrefetchScalarGridSpec(
        num_scalar_prefetch=1,
        grid=(n,),
        in_specs=[whole((n, MLA_HEADS, KV_LORA)), whole((n, MLA_HEADS, MLA_ROPE)), per((1, KV_LORA)),
                  per((1, MLA_ROPE)), pl.BlockSpec(memory_space=pl.ANY), pl.BlockSpec(memory_space=pl.ANY)],
        out_specs=per((MLA_HEADS, KV_LORA)),
        scratch_shapes=[pltpu.VMEM((PAGED_SLOTS, pc, page, KV_LORA), F32),
                        pltpu.VMEM((PAGED_SLOTS, pc, MLA_ROPE, page), F32),
                        pltpu.VMEM((2, pc * page, KV_LORA), BF16),
                        pltpu.VMEM((MLA_HEADS, pc * page), F32),
                        pltpu.SemaphoreType.DMA((2, PAGED_SLOTS))],
    )
    return pl.pallas_call(
        functools.partial(_paged_kernel, n_pages=n_pages),
        grid_spec=grid_spec,
        out_shape=jax.ShapeDtypeStruct((n, MLA_HEADS, KV_LORA), F32),
        compiler_params=_params("arbitrary"),
        name="paged_mla",
    )(page_table, qlat, qrope, ckv_new, kr_new, cache_ckv, cache_kr)


def _latv_kernel(lat_ref, wuv_ref, o_ref):
    outs = [_dot(lat_ref[hd].astype(BF16), wuv_ref[:, hd * MLA_VDIM:(hd + 1) * MLA_VDIM])
            for hd in range(MLA_HEADS)]
    o_ref[...] = jnp.concatenate(outs, axis=-1).astype(o_ref.dtype)


def _latv(lat_hb, wuv):
    n = lat_hb.shape[1]
    return pl.pallas_call(
        _latv_kernel,
        out_shape=jax.ShapeDtypeStruct((n, MLA_WIDTH), BF16),
        compiler_params=pltpu.CompilerParams(vmem_limit_bytes=VMEM_LIMIT),
        name="latv",
    )(lat_hb, wuv)


HG_SAMPLE_BLOCK = 8


def _hgrn_sample_kernel(s_ref, zh_ref, lbraw_ref, gn_ref, so_ref, o_ref):
    gn = gn_ref[...]
    nbk = HG_SAMPLE_BLOCK
    pad = jnp.zeros((HG_DK - nbk, HG_DK), F32)
    heads = []
    for hd in range(HG_HEADS):
        lb = _lower_bound(lbraw_ref[:, hd * HG_DK:(hd + 1) * HG_DK])
        f_rows = lb + (1.0 - lb) * jax.nn.sigmoid(zh_ref[:, HG_QK + hd * HG_DK:HG_QK + (hd + 1) * HG_DK])
        f_cols = jnp.concatenate([f_rows, pad], axis=0).T
        q_cols = jnp.concatenate([zh_ref[:, hd * HG_DK:(hd + 1) * HG_DK], pad], axis=0).T
        outs = []
        for j in range(nbk):
            f = f_cols[:, j:j + 1]
            v = zh_ref[j:j + 1, 2 * HG_QK + hd * HG_DV:2 * HG_QK + (hd + 1) * HG_DV]
            s_new = f * s_ref[j, hd] + (1.0 - f) * v
            so_ref[j, hd] = s_new
            o = jnp.sum(q_cols[:, j:j + 1] * s_new, axis=0, keepdims=True)
            outs.append(_rms(o, gn))
        heads.append(jnp.concatenate(outs, axis=0))
    o = jnp.concatenate(heads, axis=-1)
    gate = zh_ref[:, 2 * HG_QK + HG_WIDTH:]
    o_ref[...] = (o * _silu(gate)).astype(o_ref.dtype)


def _hgrn_sample(state, zh, lb_raw, gn):
    n = state.shape[0]
    nb = HG_SAMPLE_BLOCK
    blk4 = pl.BlockSpec((nb, HG_HEADS, HG_DK, HG_DV), lambda i: (i, 0, 0, 0))
    return pl.pallas_call(
        _hgrn_sample_kernel,
        grid=(n // nb,),
        in_specs=[blk4, pl.BlockSpec((nb, zh.shape[1]), lambda i: (i, 0)),
                  _const_spec(lb_raw.shape), _const_spec((1, HG_DV))],
        out_specs=[blk4, pl.BlockSpec((nb, HG_WIDTH), lambda i: (i, 0))],
        out_shape=[jax.ShapeDtypeStruct(state.shape, F32), jax.ShapeDtypeStruct((n, HG_WIDTH), BF16)],
        compiler_params=_params("parallel"),
        name="hgrn_sample",
    )(state, zh, lb_raw, gn)


def _outproj_kernel(*refs, with_mem):
    if with_mem:
        x_ref, omla_ref, ohg_ref, wout_ref, g_ref, wmq_ref, mk_ref, mv_ref, x1_ref, qm_ref = refs
    else:
        x_ref, omla_ref, ohg_ref, wout_ref, g_ref, wmq_ref, x1_ref, qm_ref = refs
    x1 = (x_ref[...] + _dot(omla_ref[...], wout_ref[:MLA_WIDTH, :])
          + _dot(ohg_ref[...], wout_ref[MLA_WIDTH:, :]))
    x1_ref[...] = x1
    hm = _rms(x1, g_ref[...]).astype(BF16)
    qm = _dot(hm, wmq_ref[...]) * (MEM_HDIM ** -0.5)
    if not with_mem:
        qm_ref[...] = qm.astype(qm_ref.dtype)
        return
    qb = qm.astype(BF16)
    heads = [slice(hd * MEM_HDIM, (hd + 1) * MEM_HDIM) for hd in range(MEM_HEADS)]
    scores = [_dot_nt(qb[:, sl], mk_ref[:, sl]) for sl in heads]
    outs = []
    for s, sl in zip(scores, heads):
        p = jnp.exp(s - jnp.max(s, axis=-1, keepdims=True))
        o = _dot(p.astype(BF16), mv_ref[:, sl])
        outs.append(o / jnp.sum(p, axis=-1, keepdims=True))
    qm_ref[...] = jnp.concatenate(outs, axis=-1).astype(qm_ref.dtype)


def _outproj(x, omla, ohg, w, tm, q_dtype, mem=None, seq_len=None):
    rows = x.shape[0]
    row = lambda wd: pl.BlockSpec((tm, wd), lambda i: (i, 0))
    in_specs = [row(D_MODEL), row(MLA_WIDTH), row(HG_WIDTH), _const_spec((MLA_WIDTH + HG_WIDTH, D_MODEL)),
                _const_spec((1, D_MODEL)), _const_spec((D_MODEL, MEM_WIDTH))]
    args = [x, omla, ohg, w["w_out"], w["norm_memx"], w["w_mq"]]
    if mem is not None:
        tiles_per_seq = seq_len // tm
        m_tok = mem[0].shape[1]
        in_specs += [pl.BlockSpec((None, m_tok, MEM_WIDTH), lambda i: (i // tiles_per_seq, 0, 0))] * 2
        args += list(mem)
    return pl.pallas_call(
        functools.partial(_outproj_kernel, with_mem=mem is not None),
        grid=(rows // tm,),
        in_specs=in_specs,
        out_specs=[row(D_MODEL), row(MEM_WIDTH)],
        out_shape=[jax.ShapeDtypeStruct((rows, D_MODEL), F32), jax.ShapeDtypeStruct((rows, MEM_WIDTH), q_dtype)],
        compiler_params=_params("parallel"),
        name="outproj_mem" if mem is not None else "outproj",
    )(*args)


def _memattn_sample_kernel(q_ref, k_ref, v_ref, o_ref):
    nblk, rows = k_ref.shape[:2]
    hid = lax.broadcasted_iota(jnp.int32, (SUBLANE, rows), 0)
    rhd = lax.broadcasted_iota(jnp.int32, (SUBLANE, rows), 1) % MEM_HEADS
    own = hid == rhd
    zero = jnp.zeros((SUBLANE - MEM_HEADS, MEM_HDIM), F32)
    scores = []
    for j in range(nblk):
        q = jnp.concatenate([q_ref[j, :, hd * MEM_HDIM:(hd + 1) * MEM_HDIM] for hd in range(MEM_HEADS)] + [zero],
                            axis=0)
        scores.append(_dot_nt(q.astype(BF16), k_ref[j].astype(BF16)))
    for j in range(nblk):
        s = jnp.where(own, scores[j], NEG_BIG)
        p = jnp.exp(s - jnp.max(s, axis=-1, keepdims=True))
        p = jnp.where(own, p, 0.0)
        p = p / jnp.sum(p, axis=-1, keepdims=True)
        of = _dot(p.astype(BF16), v_ref[j].astype(BF16))
        o_ref[j] = jnp.concatenate([of[hd:hd + 1] for hd in range(MEM_HEADS)], axis=-1).astype(o_ref.dtype)


MEM_SAMPLE_BLOCK = 4


def _memattn_sample(q, k, v):
    n, m = k.shape[:2]
    sb = MEM_SAMPLE_BLOCK
    return pl.pallas_call(
        _memattn_sample_kernel,
        grid=(n // sb,),
        in_specs=[pl.BlockSpec((sb, 1, MEM_WIDTH), lambda i: (i, 0, 0)),
                  pl.BlockSpec((sb, m, MEM_HDIM), lambda i: (i, 0, 0)),
                  pl.BlockSpec((sb, m, MEM_HDIM), lambda i: (i, 0, 0))],
        out_specs=pl.BlockSpec((sb, 1, MEM_WIDTH), lambda i: (i, 0, 0)),
        out_shape=jax.ShapeDtypeStruct((n, 1, MEM_WIDTH), BF16),
        compiler_params=_params("parallel"),
        name="memattn_sample",
    )(q, k, v)


def _ffn_kernel(*refs, seq_mode, tiles_per_seq):
    if seq_mode:
        (x1_ref, om_ref, wmo_ref, g_ref, wup_ref, cw_ref, cb_ref, wdn_ref, gfin_ref,
         y_ref, tail_ref, carry_ref) = refs
    else:
        (x1_ref, om_ref, wmo_ref, g_ref, wup_ref, cw_ref, cb_ref, wdn_ref, gfin_ref, prev2_ref, prev1_ref,
         y_ref, tail_ref) = refs
    tm = x1_ref.shape[0]
    x2 = x1_ref[...] + _dot(om_ref[...], wmo_ref[...])
    h = _rms(x2, g_ref[...]).astype(BF16)
    acc = x2
    if seq_mode:
        first = pl.program_id(0) % tiles_per_seq == 0
        rid = lax.broadcasted_iota(jnp.int32, (tm, 1), 0)

        @pl.when(pl.program_id(0) == 0)
        def _():
            carry_ref[...] = jnp.zeros_like(carry_ref)
    ups = [(_dot(h, wup_ref[:, f0:f0 + FFN_CHUNK]), _dot(h, wup_ref[:, D_FF + f0:D_FF + f0 + FFN_CHUNK]))
           for f0 in range(0, D_FF, FFN_CHUNK)]
    for (a, gt), f0 in zip(ups, range(0, D_FF, FFN_CHUNK)):
        fs = slice(f0, f0 + FFN_CHUNK)
        if seq_mode:
            prev = jnp.where(first, 0.0, carry_ref[:, fs])
            p1 = prev[SUBLANE - 1:SUBLANE]
            p2 = prev[SUBLANE - 2:SUBLANE - 1]
            a1 = jnp.where(rid == 0, p1, pltpu.roll(a, 1, axis=0))
            a2 = jnp.where(rid == 0, p2, jnp.where(rid == 1, p1, pltpu.roll(a, 2, axis=0)))
            carry_ref[:, fs] = a[tm - SUBLANE:, :]
            tail_ref[:, fs] = a[tm - (CONV_W - 1):, :]
        else:
            a1 = prev1_ref[:, fs]
            a2 = prev2_ref[:, fs]
            tail_ref[:, fs] = a
        conv = cb_ref[:, fs] + a2 * cw_ref[0:1, fs] + a1 * cw_ref[1:2, fs] + a * cw_ref[2:3, fs]
        u = (_silu(conv) * gt).astype(BF16)
        acc = acc + _dot(u, wdn_ref[fs, :])
    y_ref[...] = _rms(acc, gfin_ref[...])


def _ffn(x1, om, w, tm, seq_len=None, prev=None):
    rows = x1.shape[0]
    seq_mode = prev is None
    row = lambda wd: pl.BlockSpec((tm, wd), lambda i: (i, 0))
    in_specs = [row(D_MODEL), row(MEM_WIDTH), _const_spec((MEM_WIDTH, D_MODEL)), _const_spec((1, D_MODEL)),
                _const_spec((D_MODEL, 2 * D_FF)), _const_spec((CONV_W, D_FF)), _const_spec((1, D_FF)),
                _const_spec((D_FF, D_MODEL)), _const_spec((1, D_MODEL))]
    args = [x1, om, w["w_mo"], w["norm_ffn"], w["w_up"], w["conv_w"], w["conv_b"], w["w_down"], w["norm_final"]]
    if seq_mode:
        tiles_per_seq = seq_len // tm
        n_seq = rows // seq_len
        tail_spec = pl.BlockSpec((None, CONV_W - 1, D_FF), lambda i: (i // tiles_per_seq, 0, 0))
        tail_shape = jax.ShapeDtypeStruct((n_seq, CONV_W - 1, D_FF), F32)
        scratch = [pltpu.VMEM((SUBLANE, D_FF), F32)]
        sem = "arbitrary"
    else:
        tiles_per_seq = 1
        in_specs += [row(D_FF), row(D_FF)]
        args += list(prev)
        tail_spec = row(D_FF)
        tail_shape = jax.ShapeDtypeStruct((rows, D_FF), F32)
        scratch = []
        sem = "parallel"
    return pl.pallas_call(
        functools.partial(_ffn_kernel, seq_mode=seq_mode, tiles_per_seq=tiles_per_seq),
        grid=(rows // tm,),
        in_specs=in_specs,
        out_specs=[row(D_MODEL), tail_spec],
        out_shape=[jax.ShapeDtypeStruct((rows, D_MODEL), F32), tail_shape],
        scratch_shapes=scratch,
        compiler_params=_params(sem),
        name="ffn_seq" if seq_mode else "ffn_step",
    )(*args)


def _rope_tables(pos):
    half = ROPE_HALF
    inv = ROPE_BASE ** (-jnp.arange(half, dtype=F32) / half)
    ang = pos[:, None] * inv[None, :]
    cos, sin = jnp.cos(ang), jnp.sin(ang)
    n = pos.shape[0]
    one = jnp.ones((n, MLA_NOPE), F32)
    zero = jnp.zeros((n, MLA_NOPE), F32)
    pad = jnp.zeros((n, LANE - MLA_NOPE - MLA_ROPE), F32)
    cos_t = jnp.concatenate([one, cos, cos, pad], axis=-1)
    sin_t = jnp.concatenate([zero, -sin, sin, pad], axis=-1)
    return cos_t, sin_t


def _rope_tile(w_rope):
    x1, x2 = w_rope[..., :ROPE_HALF], w_rope[..., ROPE_HALF:]
    return jnp.concatenate([x1, x2, x2, x1], axis=-1)


def _prep_weights(norm_mix, w_in, norm_q, norm_kv, w_uq, w_uk, w_uv, norm_hg, w_out, norm_memx, norm_mem,
                  w_mq, w_mk, w_mv, w_mo, norm_ffn, w_up, conv_w, conv_b, w_down, norm_final):
    off_kr = Q_LORA + KV_LORA
    off_hg = off_kr + MLA_ROPE
    kr_tile = jnp.concatenate([jnp.zeros((D_MODEL, MLA_NOPE), F32), _rope_tile(w_in[:, off_kr:off_hg])], axis=-1)
    w_in_pad = jnp.concatenate([w_in[:, :off_kr], kr_tile, w_in[:, off_hg:]], axis=-1)
    uq = jnp.concatenate([w_uq[..., :MLA_NOPE], _rope_tile(w_uq[..., MLA_NOPE:])], axis=-1)
    uk = jnp.concatenate([w_uk, jnp.zeros((KV_LORA, MLA_HEADS, HEAD_PAD - MLA_NOPE), F32)], axis=-1)
    ukt = jnp.transpose(uk, (1, 2, 0))
    row = lambda g: g.reshape(1, -1)
    return {
        "norm_mix": row(norm_mix), "w_in": w_in_pad.astype(BF16), "norm_q": row(norm_q), "norm_kv": row(norm_kv),
        "w_uq": uq.reshape(Q_LORA, MLA_HEADS * HEAD_PAD).astype(BF16),
        "w_uq_t": uq.reshape(Q_LORA, MLA_HEADS * HEAD_PAD).T.astype(BF16),
        "w_uv_t": w_uv.reshape(KV_LORA, MLA_WIDTH).T.astype(BF16),
        "w_uk": uk.reshape(KV_LORA, MLA_HEADS * HEAD_PAD).astype(BF16),
        "w_ukt": ukt.astype(BF16),
        "w_uv": w_uv.reshape(KV_LORA, MLA_WIDTH).astype(BF16),
        "norm_hg": row(norm_hg), "w_out": w_out.astype(BF16), "norm_memx": row(norm_memx),
        "norm_mem": row(norm_mem), "w_mq": w_mq.reshape(D_MODEL, MEM_WIDTH).astype(BF16),
        "w_mk": w_mk.reshape(D_MODEL, MEM_WIDTH).astype(BF16), "w_mv": w_mv.reshape(D_MODEL, MEM_WIDTH).astype(BF16),
        "w_mo": w_mo.reshape(MEM_WIDTH, D_MODEL).astype(BF16), "norm_ffn": row(norm_ffn),
        "w_up": w_up.astype(BF16), "conv_w": conv_w, "conv_b": row(conv_b), "w_down": w_down.astype(BF16),
        "norm_final": row(norm_final),
    }


def kernel(x_prompt, x_sample, mem_prompt, cache_ckv, cache_krope, page_table, cache_mem_k, cache_mem_v,
           state_hgrn, state_conv, norm_mix, w_in, norm_q, norm_kv, w_uq, w_uk, w_uv, hg_lb_raw, norm_hg,
           w_out, norm_memx, norm_mem, w_mq, w_mk, w_mv, w_mo, norm_ffn, w_up, conv_w, conv_b, w_down,
           norm_final):
    bp, t, _ = x_prompt.shape
    db = x_sample.shape[0]
    n_pages = page_table.shape[1]
    page = cache_ckv.shape[2]
    past_len = n_pages * page
    w = _prep_weights(norm_mix[0], w_in[0], norm_q[0], norm_kv[0], w_uq[0], w_uk[0], w_uv[0], norm_hg[0],
                      w_out[0], norm_memx[0], norm_mem[0], w_mq[0], w_mk[0], w_mv[0], w_mo[0], norm_ffn[0],
                      w_up[0], conv_w[0], conv_b[0], w_down[0], norm_final)
    tm = ROW_TILE
    rows_p = bp * t

    mk_p, mv_p, mk_pb, mv_pb = _memkv(mem_prompt.reshape(-1, D_MODEL), w["norm_mem"], w["w_mk"], w["w_mv"])
    cos_p, sin_p = _rope_tables(jnp.arange(t, dtype=F32))
    xp = x_prompt.reshape(rows_p, D_MODEL)
    qt_p, k_p, vt_p, ckv_p, kr_p, zh_p = _mix_in(xp, w, cos_p, sin_p, tm, True)
    hq = MLA_HEADS * HEAD_PAD
    o_mla_p = _mla_prompt(qt_p, k_p.reshape(bp, t, hq), vt_p, bp, t, tm)
    o_hg_p, s_p = _hgrn_prompt(zh_p.reshape(bp, t, -1), hg_lb_raw, w["norm_hg"])
    m_tok = mem_prompt.shape[1]
    x1_p, om_p = _outproj(xp, o_mla_p.reshape(rows_p, -1), o_hg_p.reshape(rows_p, -1), w, tm, BF16,
                          mem=(mk_pb.reshape(bp, m_tok, MEM_WIDTH), mv_pb.reshape(bp, m_tok, MEM_WIDTH)), seq_len=t)
    y_p, cv_p = _ffn(x1_p, om_p, w, tm, seq_len=t)

    cos_s, sin_s = _rope_tables(jnp.full((db,), past_len, F32))
    xs = x_sample.reshape(db, D_MODEL)
    q_s, _, _, ckv_s, kr_s, zh_s = _mix_in(xs, w, cos_s, sin_s, db, False)
    qlat = jnp.transpose(_qlat(q_s, w["w_ukt"]), (1, 0, 2))
    qrope = q_s.reshape(db, MLA_HEADS, HEAD_PAD)[:, :, MLA_NOPE:MLA_NOPE + MLA_ROPE].astype(F32)
    kr_new = kr_s[:, MLA_NOPE:MLA_NOPE + MLA_ROPE]
    lat = _paged_attention(page_table, qlat, qrope, ckv_s.reshape(db, 1, KV_LORA), kr_new.reshape(db, 1, MLA_ROPE),
                           cache_ckv.reshape(-1, page, KV_LORA),
                           jnp.swapaxes(cache_krope, 2, 3).reshape(-1, MLA_ROPE, page))
    o_mla_s = _latv(jnp.transpose(lat, (1, 0, 2)), w["w_uv"])
    s_s, o_hg_s = _hgrn_sample(state_hgrn.reshape(db, HG_HEADS, HG_DK, HG_DV), zh_s, hg_lb_raw, w["norm_hg"])
    x1_s, qm_s = _outproj(xs, o_mla_s, o_hg_s, w, db, F32)
    om_s = _memattn_sample(qm_s.reshape(db, 1, MEM_WIDTH), cache_mem_k.reshape(db, m_tok * MEM_HEADS, MEM_HDIM),
                           cache_mem_v.reshape(db, m_tok * MEM_HEADS, MEM_HDIM))
    y_s, a_s = _ffn(x1_s, om_s.reshape(db, MEM_WIDTH), w, db, prev=(state_conv[0, :, 0], state_conv[0, :, 1]))
    cv_s = jnp.stack([state_conv[0, :, 1], a_s], axis=1)

    rope_sl = slice(MLA_NOPE, MLA_NOPE + MLA_ROPE)
    return (y_p.reshape(bp, t, D_MODEL), y_s.reshape(db, 1, D_MODEL),
            ckv_p.reshape(1, bp, t, KV_LORA), kr_p[:, rope_sl].reshape(1, bp, t, MLA_ROPE),
            mk_p.reshape(1, bp, m_tok, MEM_HEADS, MEM_HDIM), mv_p.reshape(1, bp, m_tok, MEM_HEADS, MEM_HDIM),
            s_p[None], cv_p[None],
            ckv_s.reshape(1, db, 1, KV_LORA), kr_new.reshape(1, db, 1, MLA_ROPE),
            s_s[None], cv_s[None])
```

```python
import functools

import jax
import jax.numpy as jnp
from jax import lax
from jax.experimental import pallas as pl
from jax.experimental.pallas import tpu as pltpu

F32 = jnp.float32
BF16 = jnp.bfloat16

D_MODEL = 1024
MLA_HEADS = 8
MLA_NOPE = 64
MLA_ROPE = 32
MLA_VDIM = 64
Q_LORA = 384
KV_LORA = 256
HG_HEADS = 4
HG_DK = 128
HG_DV = 128
HG_CHUNK = 32
HG_QK = HG_HEADS * HG_DK
HG_WIDTH = HG_HEADS * HG_DV
MLA_WIDTH = MLA_HEADS * MLA_VDIM
MEM_HEADS = 4
MEM_HDIM = 128
MEM_WIDTH = MEM_HEADS * MEM_HDIM
D_FF = 2816
CONV_W = 3
ROPE_BASE = 10000.0
EPS = 1e-6

LANE = 128
SUBLANE = 8
HEAD_PAD = LANE
ROPE_HALF = MLA_ROPE // 2
OFF_CKV = Q_LORA
OFF_KRT = OFF_CKV + KV_LORA
OFF_HGP = OFF_KRT + LANE
IN_COLS_PAD = OFF_HGP + 2 * HG_QK + 2 * HG_WIDTH
VMEM_LIMIT = 56 * 1024 * 1024
ROW_TILE = 512
FFN_CHUNK = D_FF // 2
PAGES_PER_CHUNK = 16
PAGES_PER_BLOCK = 8
PAGED_SLOTS = 4
PAGED_SAMPLES_PER_STEP = 16
HG_GROUP = 256
NEG_BIG = -1e30
LOG2_E = 1.4426950408889634


def _rms(x, g):
    return x * lax.rsqrt(jnp.mean(x * x, axis=-1, keepdims=True) + EPS) * g


def _dot(a, b):
    return jnp.dot(a, b, preferred_element_type=F32)


def _dot_nt(a, b):
    return lax.dot_general(a, b, (((1,), (1,)), ((), ())), preferred_element_type=F32)


def _dot_tn(a, b):
    return lax.dot_general(a, b, (((0,), (0,)), ((), ())), preferred_element_type=F32)


def _silu(x):
    return x * jax.nn.sigmoid(x)


def _params(*sem):
    return pltpu.CompilerParams(dimension_semantics=sem, vmem_limit_bytes=VMEM_LIMIT)


def _const_spec(shape):
    nd = len(shape)
    return pl.BlockSpec(shape, lambda *_: (0,) * nd, pipeline_mode=pl.Buffered(1))


def _memkv_kernel(mem_ref, g_ref, wk_ref, wv_ref, k_ref, v_ref, kb_ref, vb_ref):
    m = _rms(mem_ref[...], g_ref[...]).astype(BF16)
    k = _dot(m, wk_ref[...])
    v = _dot(m, wv_ref[...])
    k_ref[...] = k
    v_ref[...] = v
    kb_ref[...] = k.astype(BF16)
    vb_ref[...] = v.astype(BF16)


def _memkv(mem, g, wk, wv):
    rows = mem.shape[0]
    tm = min(ROW_TILE, rows)
    row = lambda w: pl.BlockSpec((tm, w), lambda i: (i, 0))
    return pl.pallas_call(
        _memkv_kernel,
        grid=(rows // tm,),
        in_specs=[row(D_MODEL), _const_spec((1, D_MODEL)), _const_spec((D_MODEL, MEM_WIDTH)),
                  _const_spec((D_MODEL, MEM_WIDTH))],
        out_specs=[row(MEM_WIDTH)] * 4,
        out_shape=[jax.ShapeDtypeStruct((rows, MEM_WIDTH), F32)] * 2
        + [jax.ShapeDtypeStruct((rows, MEM_WIDTH), BF16)] * 2,
        compiler_params=_params("parallel"),
        name="memkv",
    )(mem, g, wk, wv)


def _mix_in_kernel(*refs, transposed):
    if transposed:
        (x_ref, g_ref, win_ref, gq_ref, gkv_ref, wuq_ref, wuk_ref, wuv_ref, cos_ref, sin_ref, cost_ref, sint_ref,
         q_ref, k_ref, v_ref, ckv_ref, kr_ref, zh_ref) = refs
    else:
        (x_ref, g_ref, win_ref, gq_ref, gkv_ref, wuq_ref, wuk_ref, wuv_ref, cos_ref, sin_ref,
         q_ref, k_ref, v_ref, ckv_ref, kr_ref, zh_ref) = refs
    h = _rms(x_ref[...], g_ref[...]).astype(BF16)
    z = _dot(h, win_ref[...])
    c_q = _rms(z[:, :OFF_CKV], gq_ref[...]).astype(BF16)
    c_kv = _rms(z[:, OFF_CKV:OFF_KRT], gkv_ref[...])
    ckv_ref[...] = c_kv
    c_kvb = c_kv.astype(BF16)
    zh_ref[...] = z[:, OFF_HGP:]
    cos = cos_ref[...]
    sin = sin_ref[...]
    zkr = z[:, OFF_KRT:OFF_HGP]
    kr = zkr * cos + pltpu.roll(zkr, LANE - MLA_ROPE, axis=1) * sin
    kr_ref[...] = kr
    scale = (MLA_NOPE + MLA_ROPE) ** -0.5
    kn = _dot(c_kvb, wuk_ref[...])
    for hd in range(MLA_HEADS):
        sl = slice(hd * HEAD_PAD, (hd + 1) * HEAD_PAD)
        k_ref[:, sl] = (kn[:, sl] + kr).astype(BF16)
    if transposed:
        cos_q = cost_ref[...] * (scale * LOG2_E)
        sin_q = sint_ref[...] * (scale * LOG2_E)
        qt = _dot_nt(wuq_ref[...], c_q)
        for hd in range(MLA_HEADS):
            sl = slice(hd * HEAD_PAD, (hd + 1) * HEAD_PAD)
            qh = qt[sl, :]
            q_ref[sl, :] = (qh * cos_q + pltpu.roll(qh, LANE - MLA_ROPE, axis=0) * sin_q).astype(BF16)
        v_ref[...] = _dot_nt(wuv_ref[...], c_kvb).astype(BF16)
    else:
        cos_q = cos * scale
        sin_q = sin * scale
        q = _dot(c_q, wuq_ref[...])
        for hd in range(MLA_HEADS):
            sl = slice(hd * HEAD_PAD, (hd + 1) * HEAD_PAD)
            qh = q[:, sl]
            q_ref[:, sl] = (qh * cos_q + pltpu.roll(qh, LANE - MLA_ROPE, axis=1) * sin_q).astype(BF16)
        v_ref[...] = _dot(c_kvb, wuv_ref[...]).astype(BF16)


def _mix_in(x, w, cos_t, sin_t, tm, transposed):
    rows = x.shape[0]
    t_tiles = cos_t.shape[0] // tm
    row = lambda wd: pl.BlockSpec((tm, wd), lambda i: (i, 0))
    tab = pl.BlockSpec((tm, LANE), lambda i: (i % t_tiles, 0))
    hq = MLA_HEADS * HEAD_PAD
    in_specs = [row(D_MODEL), _const_spec((1, D_MODEL)), _const_spec((D_MODEL, IN_COLS_PAD)),
                _const_spec((1, Q_LORA)), _const_spec((1, KV_LORA))]
    args = [x, w["norm_mix"], w["w_in"], w["norm_q"], w["norm_kv"]]
    if transposed:
        tile_t = lambda wd: pl.BlockSpec((None, wd, tm), lambda i: (i, 0, 0))
        tab_t = pl.BlockSpec((LANE, tm), lambda i: (0, i % t_tiles))
        in_specs += [_const_spec((hq, Q_LORA)), _const_spec((KV_LORA, hq)), _const_spec((MLA_WIDTH, KV_LORA)),
                     tab, tab, tab_t, tab_t]
        args += [w["w_uq_t"], w["w_uk"], w["w_uv_t"], cos_t, sin_t, cos_t.T, sin_t.T]
        qv_specs = [tile_t(hq), row(hq), tile_t(MLA_WIDTH)]
        qv_shapes = [jax.ShapeDtypeStruct((rows // tm, hq, tm), BF16), jax.ShapeDtypeStruct((rows, hq), BF16),
                     jax.ShapeDtypeStruct((rows // tm, MLA_WIDTH, tm), BF16)]
    else:
        in_specs += [_const_spec((Q_LORA, hq)), _const_spec((KV_LORA, hq)), _const_spec((KV_LORA, MLA_WIDTH)),
                     tab, tab]
        args += [w["w_uq"], w["w_uk"], w["w_uv"], cos_t, sin_t]
        qv_specs = [row(hq), row(hq), row(MLA_WIDTH)]
        qv_shapes = [jax.ShapeDtypeStruct((rows, hq), BF16), jax.ShapeDtypeStruct((rows, hq), BF16),
                     jax.ShapeDtypeStruct((rows, MLA_WIDTH), BF16)]
    return pl.pallas_call(
        functools.partial(_mix_in_kernel, transposed=transposed),
        grid=(rows // tm,),
        in_specs=in_specs,
        out_specs=qv_specs + [row(KV_LORA), row(LANE), row(4 * HG_QK)],
        out_shape=qv_shapes + [jax.ShapeDtypeStruct((rows, KV_LORA), F32),
                               jax.ShapeDtypeStruct((rows, LANE), F32), jax.ShapeDtypeStruct((rows, 4 * HG_QK), F32)],
        compiler_params=_params("parallel"),
        name="mix_in_t" if transposed else "mix_in",
    )(*args)


HEADS_PER_STEP = 2
ATTN_KEY_TILE = 128
ATTN_QUERY_TILE = 256
ATTN_LOOKAHEAD = 12
ATTN_ONES_ROWS = 16


def _mla_prompt_kernel(qt_ref, k_ref, vt_ref, o_ref, *, bq):
    kt_n = bq // ATTN_KEY_TILE
    qh_n = bq // ATTN_QUERY_TILE
    chains = [(hh, qh) for hh in range(HEADS_PER_STEP) for qh in range(qh_n)]
    qw = ATTN_QUERY_TILE
    ones = jnp.ones((ATTN_ONES_ROWS, ATTN_KEY_TILE), BF16)

    def block(j, carry, qi, masked):
        carry = list(carry)
        ops = [(kt * ATTN_KEY_TILE, ci) for kt in range(kt_n) for ci in range(len(chains))
               if not (masked and kt * ATTN_KEY_TILE > chains[ci][1] * ATTN_QUERY_TILE + ATTN_QUERY_TILE - 1)]

        def scores(k0, ci):
            hh, qh = chains[ci]
            q0 = qh * ATTN_QUERY_TILE
            ksl = slice(hh * HEAD_PAD, (hh + 1) * HEAD_PAD)
            rows = pl.ds(pl.multiple_of(j * bq + k0, ATTN_KEY_TILE), ATTN_KEY_TILE)
            st = _dot(k_ref[rows, ksl], qt_ref[qi, ksl, q0:q0 + ATTN_QUERY_TILE])
            if masked and k0 + ATTN_KEY_TILE - 1 > q0:
                r = lax.broadcasted_iota(jnp.int32, st.shape, 0) + k0
                c = lax.broadcasted_iota(jnp.int32, st.shape, 1) + q0
                st = jnp.where(r <= c, st, NEG_BIG)
            return st

        pending = [scores(*op) for op in ops[:ATTN_LOOKAHEAD]]
        for i, (k0, ci) in enumerate(ops):
            if i + ATTN_LOOKAHEAD < len(ops):
                pending.append(scores(*ops[i + ATTN_LOOKAHEAD]))
            st = pending.pop(0)
            m, acc = carry[ci]
            hh = chains[ci][0]
            vsl = slice(hh * MLA_VDIM, (hh + 1) * MLA_VDIM)
            m_new = jnp.maximum(m, jnp.max(st, axis=0, keepdims=True))
            alpha = jnp.exp2(m - m_new)
            pt = jnp.exp2(st - m_new)
            vt_ext = jnp.concatenate([vt_ref[j, vsl, k0:k0 + ATTN_KEY_TILE], ones], axis=0)
            acc = alpha * acc + _dot(vt_ext, pt.astype(BF16))
            carry[ci] = (m_new, acc)
        return tuple(carry)

    def query_block(qi, _):
        init = tuple((jnp.full((1, qw), NEG_BIG, F32), jnp.zeros((MLA_VDIM + ATTN_ONES_ROWS, qw), F32))
                     for _ in chains)
        carry = lax.fori_loop(0, qi, functools.partial(block, qi=qi, masked=False), init)
        carry = block(qi, carry, qi, True)
        outs = [acc[:MLA_VDIM] / acc[MLA_VDIM:MLA_VDIM + 1] for _, acc in carry]
        heads = [jnp.concatenate(outs[hh * qh_n:(hh + 1) * qh_n], axis=1) for hh in range(HEADS_PER_STEP)]
        o_ref[pl.ds(pl.multiple_of(qi * bq, bq), bq), :] = jnp.concatenate(heads, axis=0).T.astype(o_ref.dtype)
        return 0

    lax.fori_loop(0, qt_ref.shape[0], query_block, 0)


def _mla_prompt(qt, k, vt, b, t, bq):
    hp = HEADS_PER_STEP
    nq = t // bq
    return pl.pallas_call(
        functools.partial(_mla_prompt_kernel, bq=bq),
        grid=(b, MLA_HEADS // hp),
        in_specs=[pl.BlockSpec((nq, hp * HEAD_PAD, bq), lambda bi, hi: (bi, hi, 0)),
                  pl.BlockSpec((None, t, hp * HEAD_PAD), lambda bi, hi: (bi, 0, hi)),
                  pl.BlockSpec((nq, hp * MLA_VDIM, bq), lambda bi, hi: (bi, hi, 0))],
        out_specs=pl.BlockSpec((None, t, hp * MLA_VDIM), lambda bi, hi: (bi, 0, hi)),
        out_shape=jax.ShapeDtypeStruct((b, t, MLA_WIDTH), BF16),
        compiler_params=_params("parallel", "parallel"),
        name="mla_prompt",
    )(qt, k, vt)


def _lower_bound(raw):
    e = jnp.exp(raw - jnp.max(raw, axis=0, keepdims=True))
    return e[0:1] / jnp.sum(e, axis=0, keepdims=True)


def _hgrn_prompt_kernel(q_ref, f_ref, i_ref, gate_ref, lbraw_ref, gn_ref, o_ref, s_ref,
                        qt_scr, o_scr, b_scr, u_scr, *, t):
    c = HG_CHUNK
    g = HG_GROUP
    cpg = g // c
    lb = _lower_bound(lbraw_ref[...])
    pos = lax.broadcasted_iota(jnp.int32, (g, HG_DK), 0) % c
    r2 = lax.broadcasted_iota(jnp.int32, (g, g), 0)
    c2 = lax.broadcasted_iota(jnp.int32, (g, g), 1)
    amask = jnp.logical_and(c2 <= r2, c2 >= r2 - r2 % c)

    def group(gi, carry):
        rows = pl.ds(pl.multiple_of(gi * g, g), g)
        q = q_ref[rows, :]
        vb = i_ref[rows, :].astype(BF16)
        f = lb + (1.0 - lb) * jax.nn.sigmoid(f_ref[rows, :])
        k = 1.0 - f
        bcum = jnp.log(f)
        for sh in (1, 2, 4, 8, 16):
            bcum = bcum + jnp.where(pos >= sh, pltpu.roll(bcum, sh, axis=0), 0.0)
        b_scr[rows, :] = bcum
        q_t = (q * jnp.exp(bcum)).astype(BF16)
        k_t = (k * jnp.exp(-bcum)).astype(BF16)
        qt_scr[rows, :] = q_t
        a = _dot_nt(q_t, k_t)
        b3 = bcum.reshape(cpg, c, HG_DK)
        k_s = (k.reshape(cpg, c, HG_DK) * jnp.exp(b3[:, c - 1:c, :] - b3)).astype(BF16)
        v3 = vb.reshape(cpg, c, HG_DV)
        for ci in range(cpg):
            u_scr[gi * cpg + ci] = _dot_tn(v3[ci], k_s[ci])
        o_scr[rows, :] = _dot(jnp.where(amask, a, 0.0).astype(BF16), vb)
        return carry

    lax.fori_loop(0, t // g, group, 0, unroll=4)

    def step(ci, st):
        rows = pl.ds(pl.multiple_of(ci * c, c), c)
        o_scr[rows, :] = o_scr[rows, :] + _dot_nt(qt_scr[rows, :], st.astype(BF16))
        decay = jnp.exp(b_scr[pl.ds(ci * c + c - 1, 1), :])
        return st * decay + u_scr[ci]

    st = lax.fori_loop(0, t // c, step, jnp.zeros((HG_DV, HG_DK), F32), unroll=32)
    s_ref[...] = st.T

    gn = gn_ref[...]

    def finish(gi, carry):
        rows = pl.ds(pl.multiple_of(gi * g, g), g)
        o_ref[rows, :] = (_rms(o_scr[rows, :], gn) * _silu(gate_ref[rows, :])).astype(o_ref.dtype)
        return carry

    lax.fori_loop(0, t // g, finish, 0, unroll=4)


def _hgrn_prompt(zh, lb_raw, gn):
    b, t, _ = zh.shape
    col = lambda off: pl.BlockSpec((None, t, HG_DK), lambda bi, hi: (bi, 0, off + hi))
    return pl.pallas_call(
        functools.partial(_hgrn_prompt_kernel, t=t),
        grid=(b, HG_HEADS),
        in_specs=[col(0), col(HG_HEADS), col(2 * HG_HEADS), col(3 * HG_HEADS),
                  pl.BlockSpec((lb_raw.shape[0], HG_DK), lambda bi, hi: (0, hi)), _const_spec((1, HG_DV))],
        out_specs=[pl.BlockSpec((None, t, HG_DV), lambda bi, hi: (bi, 0, hi)),
                   pl.BlockSpec((None, None, HG_DK, HG_DV), lambda bi, hi: (bi, hi, 0, 0))],
        out_shape=[jax.ShapeDtypeStruct((b, t, HG_WIDTH), BF16),
                   jax.ShapeDtypeStruct((b, HG_HEADS, HG_DK, HG_DV), F32)],
        scratch_shapes=[pltpu.VMEM((t, HG_DK), BF16), pltpu.VMEM((t, HG_DV), F32), pltpu.VMEM((t, HG_DK), F32),
                        pltpu.VMEM((t // HG_CHUNK, HG_DV, HG_DK), F32)],
        compiler_params=_params("parallel", "parallel"),
        name="hgrn_prompt",
    )(zh, zh, zh, zh, lb_raw, gn)


def _qlat_kernel(q_ref, wukt_ref, o_ref):
    for hd in range(MLA_HEADS):
        o_ref[hd] = _dot(q_ref[:, hd * HEAD_PAD:(hd + 1) * HEAD_PAD], wukt_ref[hd])


def _qlat(q_pad, wukt):
    n = q_pad.shape[0]
    return pl.pallas_call(
        _qlat_kernel,
        out_shape=jax.ShapeDtypeStruct((MLA_HEADS, n, KV_LORA), F32),
        compiler_params=pltpu.CompilerParams(vmem_limit_bytes=VMEM_LIMIT),
        name="qlat",
    )(q_pad, wukt)


def _paged_kernel(pt_ref, qlat_ref, qrope_ref, ckvn_ref, krn_ref, ckv_hbm, krt_hbm, lat_ref,
                  kvbuf, krbuf, kvb, s_scr, sem, *, n_pages):
    nb = lat_ref.shape[0]
    pc = PAGES_PER_CHUNK
    pb = PAGES_PER_BLOCK
    ns = PAGED_SLOTS
    ahead = ns - 1
    n_chunks = n_pages // pc
    page = kvbuf.shape[2]

    def copies(bb, ch, slot):
        out = []
        for p in range(pc):
            pg = pt_ref[bb, ch * pc + p]
            out.append(pltpu.make_async_copy(ckv_hbm.at[pg], kvbuf.at[slot, p], sem.at[0, slot]))
            out.append(pltpu.make_async_copy(krt_hbm.at[pg], krbuf.at[slot, p], sem.at[1, slot]))
        return out

    def start(bb, ch, slot):
        for cp in copies(bb, ch, slot):
            cp.start()

    def locate(b, ch_in_b):
        over = ch_in_b // n_chunks
        return jnp.minimum(b + over, nb - 1), ch_in_b - over * n_chunks

    def prefetch(b, ch_in_b):
        tb, tch = locate(b, ch_in_b + ahead)
        start(tb, tch, (ch_in_b + ahead) % ns)

    def load_and_score(b, ch_in_b):
        tb, tch = locate(b, ch_in_b)
        slot = ch_in_b % ns
        for cp in copies(tb, tch, slot):
            cp.wait()
        par = ch_in_b % 2
        qlat_b = qlat_ref[tb].astype(BF16)
        qrope_b = qrope_ref[tb].astype(BF16)
        scores = []
        for j in range(pc // pb):
            kv = kvbuf[slot, pl.ds(j * pb, pb)].reshape(pb * page, KV_LORA).astype(BF16)
            kvb[par, pl.ds(j * pb * page, pb * page), :] = kv
            krt = jnp.concatenate([krbuf[slot, j * pb + p] for p in range(pb)], axis=1).astype(BF16)
            scores.append(_dot_nt(qlat_b, kv) + _dot(qrope_b, krt))
        return jnp.concatenate(scores, axis=1)

    step = pl.program_id(0)
    per_step = nb // pl.num_programs(0)

    @pl.when(step == 0)
    def _():
        for ch0 in range(ahead):
            start(0, ch0, ch0)
        prefetch(0, 0)
        s_scr[...] = load_and_score(0, 0)

    def sample(u, s):
        b = step * per_step + u
        def chunk(ch, carry):
            s, m, l, acc = carry
            prefetch(b, ch + 1)
            s_next = load_and_score(b, ch + 1)
            m_new = jnp.maximum(m, jnp.max(s, axis=-1, keepdims=True))
            alpha = jnp.exp(m - m_new)
            p = jnp.exp(s - m_new)
            l = alpha * l + jnp.sum(p, axis=-1, keepdims=True)
            acc = alpha * acc + _dot(p.astype(BF16), kvb[ch % 2])
            return s_next, m_new, l, acc

        init = (s, jnp.full((MLA_HEADS, 1), NEG_BIG, F32), jnp.zeros((MLA_HEADS, 1), F32),
                jnp.zeros((MLA_HEADS, KV_LORA), F32))
        s, m, l, acc = lax.fori_loop(0, n_chunks, chunk, init)

        qlat = qlat_ref[b]
        qrope = qrope_ref[b]
        ckvn = ckvn_ref[b]
        s_new = (jnp.sum(qlat * ckvn, axis=-1, keepdims=True)
                 + jnp.sum(qrope * krn_ref[b], axis=-1, keepdims=True))
        m_new = jnp.maximum(m, s_new)
        alpha = jnp.exp(m - m_new)
        p_new = jnp.exp(s_new - m_new)
        l = alpha * l + p_new
        lat_ref[b] = (alpha * acc + p_new * ckvn) / l
        return s

    s_scr[...] = lax.fori_loop(0, per_step, sample, s_scr[...])

    @pl.when(step == pl.num_programs(0) - 1)
    def _():
        for i in range(1, ahead + 1):
            tb, tch = locate(nb - 1, n_chunks + i)
            for cp in copies(tb, tch, (n_chunks + i) % ns):
                cp.wait()


def _paged_attention(page_table, qlat, qrope, ckv_new, kr_new, cache_ckv, cache_kr):
    n, n_pages = page_table.shape
    page = cache_ckv.shape[1]
    pc = PAGES_PER_CHUNK
    whole = lambda shape: pl.BlockSpec(shape, lambda i, pt: (0, 0, 0))
    grid_spec = pltpu.PrefetchScalarGridSpec(
        num_scalar_prefetch=1,
        grid=(n // PAGED_SAMPLES_PER_STEP,),
        in_specs=[whole((n, MLA_HEADS, KV_LORA)), whole((n, MLA_HEADS, MLA_ROPE)), whole((n, 1, KV_LORA)),
                  whole((n, 1, MLA_ROPE)), pl.BlockSpec(memory_space=pl.ANY), pl.BlockSpec(memory_space=pl.ANY)],
        out_specs=whole((n, MLA_HEADS, KV_LORA)),
        scratch_shapes=[pltpu.VMEM((PAGED_SLOTS, pc, page, KV_LORA), F32),
                        pltpu.VMEM((PAGED_SLOTS, pc, MLA_ROPE, page), F32),
                        pltpu.VMEM((2, pc * page, KV_LORA), BF16),
                        pltpu.VMEM((MLA_HEADS, pc * page), F32),
                        pltpu.SemaphoreType.DMA((2, PAGED_SLOTS))],
    )
    return pl.pallas_call(
        functools.partial(_paged_kernel, n_pages=n_pages),
        grid_spec=grid_spec,
        out_shape=jax.ShapeDtypeStruct((n, MLA_HEADS, KV_LORA), F32),
        compiler_params=_params("arbitrary"),
        name="paged_mla",
    )(page_table, qlat, qrope, ckv_new, kr_new, cache_ckv, cache_kr)


def _latv_kernel(lat_ref, wuv_ref, o_ref):
    outs = [_dot(lat_ref[hd].astype(BF16), wuv_ref[:, hd * MLA_VDIM:(hd + 1) * MLA_VDIM])
            for hd in range(MLA_HEADS)]
    o_ref[...] = jnp.concatenate(outs, axis=-1).astype(o_ref.dtype)


def _latv(lat_hb, wuv):
    n = lat_hb.shape[1]
    return pl.pallas_call(
        _latv_kernel,
        out_shape=jax.ShapeDtypeStruct((n, MLA_WIDTH), BF16),
        compiler_params=pltpu.CompilerParams(vmem_limit_bytes=VMEM_LIMIT),
        name="latv",
    )(lat_hb, wuv)


HG_SAMPLE_BLOCK = 8


def _hgrn_sample_kernel(s_ref, zh_ref, lbraw_ref, gn_ref, so_ref, o_ref):
    gn = gn_ref[...]
    nbk = HG_SAMPLE_BLOCK
    pad = jnp.zeros((HG_DK - nbk, HG_DK), F32)
    heads = []
    for hd in range(HG_HEADS):
        lb = _lower_bound(lbraw_ref[:, hd * HG_DK:(hd + 1) * HG_DK])
        f_rows = lb + (1.0 - lb) * jax.nn.sigmoid(zh_ref[:, HG_QK + hd * HG_DK:HG_QK + (hd + 1) * HG_DK])
        f_cols = jnp.concatenate([f_rows, pad], axis=0).T
        q_cols = jnp.concatenate([zh_ref[:, hd * HG_DK:(hd + 1) * HG_DK], pad], axis=0).T
        outs = []
        for j in range(nbk):
            f = f_cols[:, j:j + 1]
            v = zh_ref[j:j + 1, 2 * HG_QK + hd * HG_DV:2 * HG_QK + (hd + 1) * HG_DV]
            s_new = f * s_ref[j, hd] + (1.0 - f) * v
            so_ref[j, hd] = s_new
            o = jnp.sum(q_cols[:, j:j + 1] * s_new, axis=0, keepdims=True)
            outs.append(_rms(o, gn))
        heads.append(jnp.concatenate(outs, axis=0))
    o = jnp.concatenate(heads, axis=-1)
    gate = zh_ref[:, 2 * HG_QK + HG_WIDTH:]
    o_ref[...] = (o * _silu(gate)).astype(o_ref.dtype)


def _hgrn_sample(state, zh, lb_raw, gn):
    n = state.shape[0]
    nb = HG_SAMPLE_BLOCK
    blk4 = pl.BlockSpec((nb, HG_HEADS, HG_DK, HG_DV), lambda i: (i, 0, 0, 0))
    return pl.pallas_call(
        _hgrn_sample_kernel,
        grid=(n // nb,),
        in_specs=[blk4, pl.BlockSpec((nb, zh.shape[1]), lambda i: (i, 0)),
                  _const_spec(lb_raw.shape), _const_spec((1, HG_DV))],
        out_specs=[blk4, pl.BlockSpec((nb, HG_WIDTH), lambda i: (i, 0))],
        out_shape=[jax.ShapeDtypeStruct(state.shape, F32), jax.ShapeDtypeStruct((n, HG_WIDTH), BF16)],
        compiler_params=_params("parallel"),
        name="hgrn_sample",
    )(state, zh, lb_raw, gn)


def _outproj_kernel(*refs, with_mem):
    if with_mem:
        x_ref, omla_ref, ohg_ref, wout_ref, g_ref, wmq_ref, mk_ref, mv_ref, x1_ref, qm_ref = refs
    else:
        x_ref, omla_ref, ohg_ref, wout_ref, g_ref, wmq_ref, x1_ref, qm_ref = refs
    x1 = (x_ref[...] + _dot(omla_ref[...], wout_ref[:MLA_WIDTH, :])
          + _dot(ohg_ref[...], wout_ref[MLA_WIDTH:, :]))
    x1_ref[...] = x1
    hm = _rms(x1, g_ref[...]).astype(BF16)
    qm = _dot(hm, wmq_ref[...]) * (MEM_HDIM ** -0.5)
    if not with_mem:
        qm_ref[...] = qm.astype(qm_ref.dtype)
        return
    qb = qm.astype(BF16)
    heads = [slice(hd * MEM_HDIM, (hd + 1) * MEM_HDIM) for hd in range(MEM_HEADS)]
    scores = [_dot_nt(qb[:, sl], mk_ref[:, sl]) for sl in heads]
    outs = []
    for s, sl in zip(scores, heads):
        p = jnp.exp(s - jnp.max(s, axis=-1, keepdims=True))
        o = _dot(p.astype(BF16), mv_ref[:, sl])
        outs.append(o / jnp.sum(p, axis=-1, keepdims=True))
    qm_ref[...] = jnp.concatenate(outs, axis=-1).astype(qm_ref.dtype)


def _outproj(x, omla, ohg, w, tm, q_dtype, mem=None, seq_len=None):
    rows = x.shape[0]
    row = lambda wd: pl.BlockSpec((tm, wd), lambda i: (i, 0))
    in_specs = [row(D_MODEL), row(MLA_WIDTH), row(HG_WIDTH), _const_spec((MLA_WIDTH + HG_WIDTH, D_MODEL)),
                _const_spec((1, D_MODEL)), _const_spec((D_MODEL, MEM_WIDTH))]
    args = [x, omla, ohg, w["w_out"], w["norm_memx"], w["w_mq"]]
    if mem is not None:
        tiles_per_seq = seq_len // tm
        m_tok = mem[0].shape[1]
        in_specs += [pl.BlockSpec((None, m_tok, MEM_WIDTH), lambda i: (i // tiles_per_seq, 0, 0))] * 2
        args += list(mem)
    return pl.pallas_call(
        functools.partial(_outproj_kernel, with_mem=mem is not None),
        grid=(rows // tm,),
        in_specs=in_specs,
        out_specs=[row(D_MODEL), row(MEM_WIDTH)],
        out_shape=[jax.ShapeDtypeStruct((rows, D_MODEL), F32), jax.ShapeDtypeStruct((rows, MEM_WIDTH), q_dtype)],
        compiler_params=_params("parallel"),
        name="outproj_mem" if mem is not None else "outproj",
    )(*args)


def _memattn_sample_kernel(q_ref, k_ref, v_ref, o_ref):
    nblk, rows = k_ref.shape[:2]
    hid = lax.broadcasted_iota(jnp.int32, (SUBLANE, rows), 0)
    rhd = lax.broadcasted_iota(jnp.int32, (SUBLANE, rows), 1) % MEM_HEADS
    own = hid == rhd
    zero = jnp.zeros((SUBLANE - MEM_HEADS, MEM_HDIM), F32)
    scores = []
    for j in range(nblk):
        q = jnp.concatenate([q_ref[j, :, hd * MEM_HDIM:(hd + 1) * MEM_HDIM] for hd in range(MEM_HEADS)] + [zero],
                            axis=0)
        scores.append(_dot_nt(q.astype(BF16), k_ref[j].astype(BF16)))
    for j in range(nblk):
        s = jnp.where(own, scores[j], NEG_BIG)
        p = jnp.exp(s - jnp.max(s, axis=-1, keepdims=True))
        p = jnp.where(own, p, 0.0)
        p = p / jnp.sum(p, axis=-1, keepdims=True)
        of = _dot(p.astype(BF16), v_ref[j].astype(BF16))
        o_ref[j] = jnp.concatenate([of[hd:hd + 1] for hd in range(MEM_HEADS)], axis=-1).astype(o_ref.dtype)


MEM_SAMPLE_BLOCK = 4


def _memattn_sample(q, k, v):
    n, m = k.shape[:2]
    sb = MEM_SAMPLE_BLOCK
    return pl.pallas_call(
        _memattn_sample_kernel,
        grid=(n // sb,),
        in_specs=[pl.BlockSpec((sb, 1, MEM_WIDTH), lambda i: (i, 0, 0)),
                  pl.BlockSpec((sb, m, MEM_HDIM), lambda i: (i, 0, 0)),
                  pl.BlockSpec((sb, m, MEM_HDIM), lambda i: (i, 0, 0))],
        out_specs=pl.BlockSpec((sb, 1, MEM_WIDTH), lambda i: (i, 0, 0)),
        out_shape=jax.ShapeDtypeStruct((n, 1, MEM_WIDTH), BF16),
        compiler_params=_params("parallel"),
        name="memattn_sample",
    )(q, k, v)


def _ffn_kernel(*refs, seq_mode, tiles_per_seq):
    if seq_mode:
        (x1_ref, om_ref, wmo_ref, g_ref, wup_ref, cw_ref, cb_ref, wdn_ref, gfin_ref,
         y_ref, tail_ref, carry_ref) = refs
    else:
        (x1_ref, om_ref, wmo_ref, g_ref, wup_ref, cw_ref, cb_ref, wdn_ref, gfin_ref, prev2_ref, prev1_ref,
         y_ref, tail_ref) = refs
    tm = x1_ref.shape[0]
    x2 = x1_ref[...] + _dot(om_ref[...], wmo_ref[...])
    h = _rms(x2, g_ref[...]).astype(BF16)
    acc = x2
    if seq_mode:
        first = pl.program_id(0) % tiles_per_seq == 0
        rid = lax.broadcasted_iota(jnp.int32, (tm, 1), 0)

        @pl.when(pl.program_id(0) == 0)
        def _():
            carry_ref[...] = jnp.zeros_like(carry_ref)
    ups = [(_dot(h, wup_ref[:, f0:f0 + FFN_CHUNK]), _dot(h, wup_ref[:, D_FF + f0:D_FF + f0 + FFN_CHUNK]))
           for f0 in range(0, D_FF, FFN_CHUNK)]
    for (a, gt), f0 in zip(ups, range(0, D_FF, FFN_CHUNK)):
        fs = slice(f0, f0 + FFN_CHUNK)
        if seq_mode:
            prev = jnp.where(first, 0.0, carry_ref[:, fs])
            p1 = prev[SUBLANE - 1:SUBLANE]
            p2 = prev[SUBLANE - 2:SUBLANE - 1]
            a1 = jnp.where(rid == 0, p1, pltpu.roll(a, 1, axis=0))
            a2 = jnp.where(rid == 0, p2, jnp.where(rid == 1, p1, pltpu.roll(a, 2, axis=0)))
            carry_ref[:, fs] = a[tm - SUBLANE:, :]
            tail_ref[:, fs] = a[tm - (CONV_W - 1):, :]
        else:
            a1 = prev1_ref[:, fs]
            a2 = prev2_ref[:, fs]
            tail_ref[:, fs] = a
        conv = cb_ref[:, fs] + a2 * cw_ref[0:1, fs] + a1 * cw_ref[1:2, fs] + a * cw_ref[2:3, fs]
        u = (_silu(conv) * gt).astype(BF16)
        acc = acc + _dot(u, wdn_ref[fs, :])
    y_ref[...] = _rms(acc, gfin_ref[...])


def _ffn(x1, om, w, tm, seq_len=None, prev=None):
    rows = x1.shape[0]
    seq_mode = prev is None
    row = lambda wd: pl.BlockSpec((tm, wd), lambda i: (i, 0))
    in_specs = [row(D_MODEL), row(MEM_WIDTH), _const_spec((MEM_WIDTH, D_MODEL)), _const_spec((1, D_MODEL)),
                _const_spec((D_MODEL, 2 * D_FF)), _const_spec((CONV_W, D_FF)), _const_spec((1, D_FF)),
                _const_spec((D_FF, D_MODEL)), _const_spec((1, D_MODEL))]
    args = [x1, om, w["w_mo"], w["norm_ffn"], w["w_up"], w["conv_w"], w["conv_b"], w["w_down"], w["norm_final"]]
    if seq_mode:
        tiles_per_seq = seq_len // tm
        n_seq = rows // seq_len
        tail_spec = pl.BlockSpec((None, CONV_W - 1, D_FF), lambda i: (i // tiles_per_seq, 0, 0))
        tail_shape = jax.ShapeDtypeStruct((n_seq, CONV_W - 1, D_FF), F32)
        scratch = [pltpu.VMEM((SUBLANE, D_FF), F32)]
        sem = "arbitrary"
    else:
        tiles_per_seq = 1
        in_specs += [row(D_FF), row(D_FF)]
        args += list(prev)
        tail_spec = row(D_FF)
        tail_shape = jax.ShapeDtypeStruct((rows, D_FF), F32)
        scratch = []
        sem = "parallel"
    return pl.pallas_call(
        functools.partial(_ffn_kernel, seq_mode=seq_mode, tiles_per_seq=tiles_per_seq),
        grid=(rows // tm,),
        in_specs=in_specs,
        out_specs=[row(D_MODEL), tail_spec],
        out_shape=[jax.ShapeDtypeStruct((rows, D_MODEL), F32), tail_shape],
        scratch_shapes=scratch,
        compiler_params=_params(sem),
        name="ffn_seq" if seq_mode else "ffn_step",
    )(*args)


def _rope_tables(pos):
    half = ROPE_HALF
    inv = ROPE_BASE ** (-jnp.arange(half, dtype=F32) / half)
    ang = pos[:, None] * inv[None, :]
    cos, sin = jnp.cos(ang), jnp.sin(ang)
    n = pos.shape[0]
    one = jnp.ones((n, MLA_NOPE), F32)
    zero = jnp.zeros((n, MLA_NOPE), F32)
    pad = jnp.zeros((n, LANE - MLA_NOPE - MLA_ROPE), F32)
    cos_t = jnp.concatenate([one, cos, cos, pad], axis=-1)
    sin_t = jnp.concatenate([zero, -sin, sin, pad], axis=-1)
    return cos_t, sin_t


def _rope_tile(w_rope):
    x1, x2 = w_rope[..., :ROPE_HALF], w_rope[..., ROPE_HALF:]
    return jnp.concatenate([x1, x2, x2, x1], axis=-1)


def _prep_weights(norm_mix, w_in, norm_q, norm_kv, w_uq, w_uk, w_uv, norm_hg, w_out, norm_memx, norm_mem,
                  w_mq, w_mk, w_mv, w_mo, norm_ffn, w_up, conv_w, conv_b, w_down, norm_final):
    off_kr = Q_LORA + KV_LORA
    off_hg = off_kr + MLA_ROPE
    kr_tile = jnp.concatenate([jnp.zeros((D_MODEL, MLA_NOPE), F32), _rope_tile(w_in[:, off_kr:off_hg])], axis=-1)
    w_in_pad = jnp.concatenate([w_in[:, :off_kr], kr_tile, w_in[:, off_hg:]], axis=-1)
    uq = jnp.concatenate([w_uq[..., :MLA_NOPE], _rope_tile(w_uq[..., MLA_NOPE:])], axis=-1)
    uk = jnp.concatenate([w_uk, jnp.zeros((KV_LORA, MLA_HEADS, HEAD_PAD - MLA_NOPE), F32)], axis=-1)
    ukt = jnp.transpose(uk, (1, 2, 0))
    row = lambda g: g.reshape(1, -1)
    return {
        "norm_mix": row(norm_mix), "w_in": w_in_pad.astype(BF16), "norm_q": row(norm_q), "norm_kv": row(norm_kv),
        "w_uq": uq.reshape(Q_LORA, MLA_HEADS * HEAD_PAD).astype(BF16),
        "w_uq_t": uq.reshape(Q_LORA, MLA_HEADS * HEAD_PAD).T.astype(BF16),
        "w_uv_t": w_uv.reshape(KV_LORA, MLA_WIDTH).T.astype(BF16),
        "w_uk": uk.reshape(KV_LORA, MLA_HEADS * HEAD_PAD).astype(BF16),
        "w_ukt": ukt.astype(BF16),
        "w_uv": w_uv.reshape(KV_LORA, MLA_WIDTH).astype(BF16),
        "norm_hg": row(norm_hg), "w_out": w_out.astype(BF16), "norm_memx": row(norm_memx),
        "norm_mem": row(norm_mem), "w_mq": w_mq.reshape(D_MODEL, MEM_WIDTH).astype(BF16),
        "w_mk": w_mk.reshape(D_MODEL, MEM_WIDTH).astype(BF16), "w_mv": w_mv.reshape(D_MODEL, MEM_WIDTH).astype(BF16),
        "w_mo": w_mo.reshape(MEM_WIDTH, D_MODEL).astype(BF16), "norm_ffn": row(norm_ffn),
        "w_up": w_up.astype(BF16), "conv_w": conv_w, "conv_b": row(conv_b), "w_down": w_down.astype(BF16),
        "norm_final": row(norm_final),
    }


def kernel(x_prompt, x_sample, mem_prompt, cache_ckv, cache_krope, page_table, cache_mem_k, cache_mem_v,
           state_hgrn, state_conv, norm_mix, w_in, norm_q, norm_kv, w_uq, w_uk, w_uv, hg_lb_raw, norm_hg,
           w_out, norm_memx, norm_mem, w_mq, w_mk, w_mv, w_mo, norm_ffn, w_up, conv_w, conv_b, w_down,
           norm_final):
    bp, t, _ = x_prompt.shape
    db = x_sample.shape[0]
    n_pages = page_table.shape[1]
    page = cache_ckv.shape[2]
    past_len = n_pages * page
    w = _prep_weights(norm_mix[0], w_in[0], norm_q[0], norm_kv[0], w_uq[0], w_uk[0], w_uv[0], norm_hg[0],
                      w_out[0], norm_memx[0], norm_mem[0], w_mq[0], w_mk[0], w_mv[0], w_mo[0], norm_ffn[0],
                      w_up[0], conv_w[0], conv_b[0], w_down[0], norm_final)
    tm = ROW_TILE
    rows_p = bp * t

    mk_p, mv_p, mk_pb, mv_pb = _memkv(mem_prompt.reshape(-1, D_MODEL), w["norm_mem"], w["w_mk"], w["w_mv"])
    cos_p, sin_p = _rope_tables(jnp.arange(t, dtype=F32))
    xp = x_prompt.reshape(rows_p, D_MODEL)
    qt_p, k_p, vt_p, ckv_p, kr_p, zh_p = _mix_in(xp, w, cos_p, sin_p, tm, True)
    hq = MLA_HEADS * HEAD_PAD
    o_mla_p = _mla_prompt(qt_p, k_p.reshape(bp, t, hq), vt_p, bp, t, tm)
    o_hg_p, s_p = _hgrn_prompt(zh_p.reshape(bp, t, -1), hg_lb_raw, w["norm_hg"])
    m_tok = mem_prompt.shape[1]
    x1_p, om_p = _outproj(xp, o_mla_p.reshape(rows_p, -1), o_hg_p.reshape(rows_p, -1), w, tm, BF16,
                          mem=(mk_pb.reshape(bp, m_tok, MEM_WIDTH), mv_pb.reshape(bp, m_tok, MEM_WIDTH)), seq_len=t)
    y_p, cv_p = _ffn(x1_p, om_p, w, tm, seq_len=t)

    cos_s, sin_s = _rope_tables(jnp.full((db,), past_len, F32))
    xs = x_sample.reshape(db, D_MODEL)
    q_s, _, _, ckv_s, kr_s, zh_s = _mix_in(xs, w, cos_s, sin_s, db, False)
    qlat = jnp.transpose(_qlat(q_s, w["w_ukt"]), (1, 0, 2))
    qrope = q_s.reshape(db, MLA_HEADS, HEAD_PAD)[:, :, MLA_NOPE:MLA_NOPE + MLA_ROPE].astype(F32)
    kr_new = kr_s[:, MLA_NOPE:MLA_NOPE + MLA_ROPE]
    lat = _paged_attention(page_table, qlat, qrope, ckv_s.reshape(db, 1, KV_LORA), kr_new.reshape(db, 1, MLA_ROPE),
                           cache_ckv.reshape(-1, page, KV_LORA),
                           jnp.swapaxes(cache_krope, 2, 3).reshape(-1, MLA_ROPE, page))
    o_mla_s = _latv(jnp.transpose(lat, (1, 0, 2)), w["w_uv"])
    s_s, o_hg_s = _hgrn_sample(state_hgrn.reshape(db, HG_HEADS, HG_DK, HG_DV), zh_s, hg_lb_raw, w["norm_hg"])
    x1_s, qm_s = _outproj(xs, o_mla_s, o_hg_s, w, db, F32)
    om_s = _memattn_sample(qm_s.reshape(db, 1, MEM_WIDTH), cache_mem_k.reshape(db, m_tok * MEM_HEADS, MEM_HDIM),
                           cache_mem_v.reshape(db, m_tok * MEM_HEADS, MEM_HDIM))
    y_s, a_s = _ffn(x1_s, om_s.reshape(db, MEM_WIDTH), w, db, prev=(state_conv[0, :, 0], state_conv[0, :, 1]))
    cv_s = jnp.stack([state_conv[0, :, 1], a_s], axis=1)

    rope_sl = slice(MLA_NOPE, MLA_NOPE + MLA_ROPE)
    return (y_p.reshape(bp, t, D_MODEL), y_s.reshape(db, 1, D_MODEL),
            ckv_p.reshape(1, bp, t, KV_LORA), kr_p[:, rope_sl].reshape(1, bp, t, MLA_ROPE),
            mk_p.reshape(1, bp, m_tok, MEM_HEADS, MEM_HDIM), mv_p.reshape(1, bp, m_tok, MEM_HEADS, MEM_HDIM),
            s_p[None], cv_p[None],
            ckv_s.reshape(1, db, 1, KV_LORA), kr_new.reshape(1, db, 1, MLA_ROPE),
            s_s[None], cv_s[None])
```

```python
import functools

import jax
import jax.numpy as jnp
from jax import lax
from jax.experimental import pallas as pl
from jax.experimental.pallas import tpu as pltpu

F32 = jnp.float32
BF16 = jnp.bfloat16

D_MODEL = 1024
MLA_HEADS = 8
MLA_NOPE = 64
MLA_ROPE = 32
MLA_VDIM = 64
Q_LORA = 384
KV_LORA = 256
HG_HEADS = 4
HG_DK = 128
HG_DV = 128
HG_CHUNK = 32
HG_QK = HG_HEADS * HG_DK
HG_WIDTH = HG_HEADS * HG_DV
MLA_WIDTH = MLA_HEADS * MLA_VDIM
MEM_HEADS = 4
MEM_HDIM = 128
MEM_WIDTH = MEM_HEADS * MEM_HDIM
D_FF = 2816
CONV_W = 3
ROPE_BASE = 10000.0
EPS = 1e-6

LANE = 128
SUBLANE = 8
HEAD_PAD = LANE
ROPE_HALF = MLA_ROPE // 2
OFF_CKV = Q_LORA
OFF_KRT = OFF_CKV + KV_LORA
OFF_HGP = OFF_KRT + LANE
IN_COLS_PAD = OFF_HGP + 2 * HG_QK + 2 * HG_WIDTH
VMEM_LIMIT = 56 * 1024 * 1024
ROW_TILE = 512
FFN_CHUNK = D_FF // 2
PAGES_PER_CHUNK = 16
PAGES_PER_BLOCK = 8
PAGED_SLOTS = 8
PAGED_SAMPLES_PER_STEP = 16
HG_GROUP = 256
NEG_BIG = -1e30
LOG2_E = 1.4426950408889634


def _rms(x, g):
    return x * lax.rsqrt(jnp.mean(x * x, axis=-1, keepdims=True) + EPS) * g


def _dot(a, b):
    return jnp.dot(a, b, preferred_element_type=F32)


def _dot_nt(a, b):
    return lax.dot_general(a, b, (((1,), (1,)), ((), ())), preferred_element_type=F32)


def _dot_tn(a, b):
    return lax.dot_general(a, b, (((0,), (0,)), ((), ())), preferred_element_type=F32)


def _silu(x):
    return x * jax.nn.sigmoid(x)


def _params(*sem):
    return pltpu.CompilerParams(dimension_semantics=sem, vmem_limit_bytes=VMEM_LIMIT)


def _const_spec(shape):
    nd = len(shape)
    return pl.BlockSpec(shape, lambda *_: (0,) * nd, pipeline_mode=pl.Buffered(1))


def _memkv_kernel(mem_ref, g_ref, wk_ref, wv_ref, k_ref, v_ref, kb_ref, vb_ref):
    m = _rms(mem_ref[...], g_ref[...]).astype(BF16)
    k = _dot(m, wk_ref[...])
    v = _dot(m, wv_ref[...])
    k_ref[...] = k
    v_ref[...] = v
    kb_ref[...] = k.astype(BF16)
    vb_ref[...] = v.astype(BF16)


def _memkv(mem, g, wk, wv):
    rows = mem.shape[0]
    tm = min(ROW_TILE, rows)
    row = lambda w: pl.BlockSpec((tm, w), lambda i: (i, 0))
    return pl.pallas_call(
        _memkv_kernel,
        grid=(rows // tm,),
        in_specs=[row(D_MODEL), _const_spec((1, D_MODEL)), _const_spec((D_MODEL, MEM_WIDTH)),
                  _const_spec((D_MODEL, MEM_WIDTH))],
        out_specs=[row(MEM_WIDTH)] * 4,
        out_shape=[jax.ShapeDtypeStruct((rows, MEM_WIDTH), F32)] * 2
        + [jax.ShapeDtypeStruct((rows, MEM_WIDTH), BF16)] * 2,
        compiler_params=_params("parallel"),
        name="memkv",
    )(mem, g, wk, wv)


def _mix_in_kernel(*refs, transposed):
    if transposed:
        (x_ref, g_ref, win_ref, gq_ref, gkv_ref, wuq_ref, wuk_ref, wuv_ref, cos_ref, sin_ref, cost_ref, sint_ref,
         q_ref, k_ref, v_ref, ckv_ref, kr_ref, zh_ref) = refs
    else:
        (x_ref, g_ref, win_ref, gq_ref, gkv_ref, wuq_ref, wuk_ref, wuv_ref, cos_ref, sin_ref,
         q_ref, k_ref, v_ref, ckv_ref, kr_ref, zh_ref) = refs
    h = _rms(x_ref[...], g_ref[...]).astype(BF16)
    z = _dot(h, win_ref[...])
    c_q = _rms(z[:, :OFF_CKV], gq_ref[...]).astype(BF16)
    c_kv = _rms(z[:, OFF_CKV:OFF_KRT], gkv_ref[...])
    ckv_ref[...] = c_kv
    c_kvb = c_kv.astype(BF16)
    zh_ref[...] = z[:, OFF_HGP:]
    cos = cos_ref[...]
    sin = sin_ref[...]
    zkr = z[:, OFF_KRT:OFF_HGP]
    kr = zkr * cos + pltpu.roll(zkr, LANE - MLA_ROPE, axis=1) * sin
    kr_ref[...] = kr
    scale = (MLA_NOPE + MLA_ROPE) ** -0.5
    kn = _dot(c_kvb, wuk_ref[...])
    for hd in range(MLA_HEADS):
        sl = slice(hd * HEAD_PAD, (hd + 1) * HEAD_PAD)
        k_ref[:, sl] = (kn[:, sl] + kr).astype(BF16)
    if transposed:
        cos_q = cost_ref[...] * (scale * LOG2_E)
        sin_q = sint_ref[...] * (scale * LOG2_E)
        qt = _dot_nt(wuq_ref[...], c_q)
        for hd in range(MLA_HEADS):
            sl = slice(hd * HEAD_PAD, (hd + 1) * HEAD_PAD)
            qh = qt[sl, :]
            q_ref[sl, :] = (qh * cos_q + pltpu.roll(qh, LANE - MLA_ROPE, axis=0) * sin_q).astype(BF16)
        v_ref[...] = _dot_nt(wuv_ref[...], c_kvb).astype(BF16)
    else:
        cos_q = cos * scale
        sin_q = sin * scale
        q = _dot(c_q, wuq_ref[...])
        for hd in range(MLA_HEADS):
            sl = slice(hd * HEAD_PAD, (hd + 1) * HEAD_PAD)
            qh = q[:, sl]
            q_ref[:, sl] = (qh * cos_q + pltpu.roll(qh, LANE - MLA_ROPE, axis=1) * sin_q).astype(BF16)
        v_ref[...] = _dot(c_kvb, wuv_ref[...]).astype(BF16)


def _mix_in(x, w, cos_t, sin_t, tm, transposed):
    rows = x.shape[0]
    t_tiles = cos_t.shape[0] // tm
    row = lambda wd: pl.BlockSpec((tm, wd), lambda i: (i, 0))
    tab = pl.BlockSpec((tm, LANE), lambda i: (i % t_tiles, 0))
    hq = MLA_HEADS * HEAD_PAD
    in_specs = [row(D_MODEL), _const_spec((1, D_MODEL)), _const_spec((D_MODEL, IN_COLS_PAD)),
                _const_spec((1, Q_LORA)), _const_spec((1, KV_LORA))]
    args = [x, w["norm_mix"], w["w_in"], w["norm_q"], w["norm_kv"]]
    if transposed:
        tile_t = lambda wd: pl.BlockSpec((None, wd, tm), lambda i: (i, 0, 0))
        tab_t = pl.BlockSpec((LANE, tm), lambda i: (0, i % t_tiles))
        in_specs += [_const_spec((hq, Q_LORA)), _const_spec((KV_LORA, hq)), _const_spec((MLA_WIDTH, KV_LORA)),
                     tab, tab, tab_t, tab_t]
        args += [w["w_uq_t"], w["w_uk"], w["w_uv_t"], cos_t, sin_t, cos_t.T, sin_t.T]
        qv_specs = [tile_t(hq), row(hq), tile_t(MLA_WIDTH)]
        qv_shapes = [jax.ShapeDtypeStruct((rows // tm, hq, tm), BF16), jax.ShapeDtypeStruct((rows, hq), BF16),
                     jax.ShapeDtypeStruct((rows // tm, MLA_WIDTH, tm), BF16)]
    else:
        in_specs += [_const_spec((Q_LORA, hq)), _const_spec((KV_LORA, hq)), _const_spec((KV_LORA, MLA_WIDTH)),
                     tab, tab]
        args += [w["w_uq"], w["w_uk"], w["w_uv"], cos_t, sin_t]
        qv_specs = [row(hq), row(hq), row(MLA_WIDTH)]
        qv_shapes = [jax.ShapeDtypeStruct((rows, hq), BF16), jax.ShapeDtypeStruct((rows, hq), BF16),
                     jax.ShapeDtypeStruct((rows, MLA_WIDTH), BF16)]
    return pl.pallas_call(
        functools.partial(_mix_in_kernel, transposed=transposed),
        grid=(rows // tm,),
        in_specs=in_specs,
        out_specs=qv_specs + [row(KV_LORA), row(LANE), row(4 * HG_QK)],
        out_shape=qv_shapes + [jax.ShapeDtypeStruct((rows, KV_LORA), F32),
                               jax.ShapeDtypeStruct((rows, LANE), F32), jax.ShapeDtypeStruct((rows, 4 * HG_QK), F32)],
        compiler_params=_params("parallel"),
        name="mix_in_t" if transposed else "mix_in",
    )(*args)


HEADS_PER_STEP = 2
ATTN_KEY_TILE = 128
ATTN_QUERY_TILE = 256
ATTN_LOOKAHEAD = 12
ATTN_ONES_ROWS = 16


def _mla_prompt_kernel(qt_ref, k_ref, vt_ref, o_ref, *, bq):
    kt_n = bq // ATTN_KEY_TILE
    qh_n = bq // ATTN_QUERY_TILE
    chains = [(hh, qh) for hh in range(HEADS_PER_STEP) for qh in range(qh_n)]
    qw = ATTN_QUERY_TILE
    ones = jnp.ones((ATTN_ONES_ROWS, ATTN_KEY_TILE), BF16)

    def block(j, carry, qi, masked):
        carry = list(carry)
        ops = [(kt * ATTN_KEY_TILE, ci) for kt in range(kt_n) for ci in range(len(chains))
               if not (masked and kt * ATTN_KEY_TILE > chains[ci][1] * ATTN_QUERY_TILE + ATTN_QUERY_TILE - 1)]

        def scores(k0, ci):
            hh, qh = chains[ci]
            q0 = qh * ATTN_QUERY_TILE
            ksl = slice(hh * HEAD_PAD, (hh + 1) * HEAD_PAD)
            rows = pl.ds(pl.multiple_of(j * bq + k0, ATTN_KEY_TILE), ATTN_KEY_TILE)
            st = _dot(k_ref[rows, ksl], qt_ref[qi, ksl, q0:q0 + ATTN_QUERY_TILE])
            if masked and k0 + ATTN_KEY_TILE - 1 > q0:
                r = lax.broadcasted_iota(jnp.int32, st.shape, 0) + k0
                c = lax.broadcasted_iota(jnp.int32, st.shape, 1) + q0
                st = jnp.where(r <= c, st, NEG_BIG)
            return st

        pending = [scores(*op) for op in ops[:ATTN_LOOKAHEAD]]
        for i, (k0, ci) in enumerate(ops):
            if i + ATTN_LOOKAHEAD < len(ops):
                pending.append(scores(*ops[i + ATTN_LOOKAHEAD]))
            st = pending.pop(0)
            m, acc = carry[ci]
            hh = chains[ci][0]
            vsl = slice(hh * MLA_VDIM, (hh + 1) * MLA_VDIM)
            m_new = jnp.maximum(m, jnp.max(st, axis=0, keepdims=True))
            alpha = jnp.exp2(m - m_new)
            pt = jnp.exp2(st - m_new)
            vt_ext = jnp.concatenate([vt_ref[j, vsl, k0:k0 + ATTN_KEY_TILE], ones], axis=0)
            acc = alpha * acc + _dot(vt_ext, pt.astype(BF16))
            carry[ci] = (m_new, acc)
        return tuple(carry)

    def query_block(qi, _):
        init = tuple((jnp.full((1, qw), NEG_BIG, F32), jnp.zeros((MLA_VDIM + ATTN_ONES_ROWS, qw), F32))
                     for _ in chains)
        carry = lax.fori_loop(0, qi, functools.partial(block, qi=qi, masked=False), init)
        carry = block(qi, carry, qi, True)
        outs = [acc[:MLA_VDIM] / acc[MLA_VDIM:MLA_VDIM + 1] for _, acc in carry]
        heads = [jnp.concatenate(outs[hh * qh_n:(hh + 1) * qh_n], axis=1) for hh in range(HEADS_PER_STEP)]
        o_ref[pl.ds(pl.multiple_of(qi * bq, bq), bq), :] = jnp.concatenate(heads, axis=0).T.astype(o_ref.dtype)
        return 0

    lax.fori_loop(0, qt_ref.shape[0], query_block, 0)


def _mla_prompt(qt, k, vt, b, t, bq):
    hp = HEADS_PER_STEP
    nq = t // bq
    return pl.pallas_call(
        functools.partial(_mla_prompt_kernel, bq=bq),
        grid=(b, MLA_HEADS // hp),
        in_specs=[pl.BlockSpec((nq, hp * HEAD_PAD, bq), lambda bi, hi: (bi, hi, 0)),
                  pl.BlockSpec((None, t, hp * HEAD_PAD), lambda bi, hi: (bi, 0, hi)),
                  pl.BlockSpec((nq, hp * MLA_VDIM, bq), lambda bi, hi: (bi, hi, 0))],
        out_specs=pl.BlockSpec((None, t, hp * MLA_VDIM), lambda bi, hi: (bi, 0, hi)),
        out_shape=jax.ShapeDtypeStruct((b, t, MLA_WIDTH), BF16),
        compiler_params=_params("parallel", "parallel"),
        name="mla_prompt",
    )(qt, k, vt)


def _lower_bound(raw):
    e = jnp.exp(raw - jnp.max(raw, axis=0, keepdims=True))
    return e[0:1] / jnp.sum(e, axis=0, keepdims=True)


def _hgrn_prompt_kernel(q_ref, f_ref, i_ref, gate_ref, lbraw_ref, gn_ref, o_ref, s_ref,
                        qt_scr, o_scr, b_scr, u_scr, *, t):
    c = HG_CHUNK
    g = HG_GROUP
    cpg = g // c
    lb = _lower_bound(lbraw_ref[...])
    pos = lax.broadcasted_iota(jnp.int32, (g, HG_DK), 0) % c
    r2 = lax.broadcasted_iota(jnp.int32, (g, g), 0)
    c2 = lax.broadcasted_iota(jnp.int32, (g, g), 1)
    amask = jnp.logical_and(c2 <= r2, c2 >= r2 - r2 % c)

    def group(gi, carry):
        rows = pl.ds(pl.multiple_of(gi * g, g), g)
        q = q_ref[rows, :]
        vb = i_ref[rows, :].astype(BF16)
        f = lb + (1.0 - lb) * jax.nn.sigmoid(f_ref[rows, :])
        k = 1.0 - f
        bcum = jnp.log(f)
        for sh in (1, 2, 4, 8, 16):
            bcum = bcum + jnp.where(pos >= sh, pltpu.roll(bcum, sh, axis=0), 0.0)
        b_scr[rows, :] = bcum
        q_t = (q * jnp.exp(bcum)).astype(BF16)
        k_t = (k * jnp.exp(-bcum)).astype(BF16)
        qt_scr[rows, :] = q_t
        a = _dot_nt(q_t, k_t)
        b3 = bcum.reshape(cpg, c, HG_DK)
        k_s = (k.reshape(cpg, c, HG_DK) * jnp.exp(b3[:, c - 1:c, :] - b3)).astype(BF16)
        v3 = vb.reshape(cpg, c, HG_DV)
        for ci in range(cpg):
            u_scr[gi * cpg + ci] = _dot_tn(v3[ci], k_s[ci])
        o_scr[rows, :] = _dot(jnp.where(amask, a, 0.0).astype(BF16), vb)
        return carry

    lax.fori_loop(0, t // g, group, 0, unroll=4)

    def step(ci, st):
        rows = pl.ds(pl.multiple_of(ci * c, c), c)
        o_scr[rows, :] = o_scr[rows, :] + _dot_nt(qt_scr[rows, :], st.astype(BF16))
        decay = jnp.exp(b_scr[pl.ds(ci * c + c - 1, 1), :])
        return st * decay + u_scr[ci]

    st = lax.fori_loop(0, t // c, step, jnp.zeros((HG_DV, HG_DK), F32), unroll=32)
    s_ref[...] = st.T

    gn = gn_ref[...]

    def finish(gi, carry):
        rows = pl.ds(pl.multiple_of(gi * g, g), g)
        o_ref[rows, :] = (_rms(o_scr[rows, :], gn) * _silu(gate_ref[rows, :])).astype(o_ref.dtype)
        return carry

    lax.fori_loop(0, t // g, finish, 0, unroll=4)


def _hgrn_prompt(zh, lb_raw, gn):
    b, t, _ = zh.shape
    col = lambda off: pl.BlockSpec((None, t, HG_DK), lambda bi, hi: (bi, 0, off + hi))
    return pl.pallas_call(
        functools.partial(_hgrn_prompt_kernel, t=t),
        grid=(b, HG_HEADS),
        in_specs=[col(0), col(HG_HEADS), col(2 * HG_HEADS), col(3 * HG_HEADS),
                  pl.BlockSpec((lb_raw.shape[0], HG_DK), lambda bi, hi: (0, hi)), _const_spec((1, HG_DV))],
        out_specs=[pl.BlockSpec((None, t, HG_DV), lambda bi, hi: (bi, 0, hi)),
                   pl.BlockSpec((None, None, HG_DK, HG_DV), lambda bi, hi: (bi, hi, 0, 0))],
        out_shape=[jax.ShapeDtypeStruct((b, t, HG_WIDTH), BF16),
                   jax.ShapeDtypeStruct((b, HG_HEADS, HG_DK, HG_DV), F32)],
        scratch_shapes=[pltpu.VMEM((t, HG_DK), BF16), pltpu.VMEM((t, HG_DV), F32), pltpu.VMEM((t, HG_DK), F32),
                        pltpu.VMEM((t // HG_CHUNK, HG_DV, HG_DK), F32)],
        compiler_params=_params("parallel", "parallel"),
        name="hgrn_prompt",
    )(zh, zh, zh, zh, lb_raw, gn)


def _qlat_kernel(q_ref, wukt_ref, o_ref):
    for hd in range(MLA_HEADS):
        o_ref[hd] = _dot(q_ref[:, hd * HEAD_PAD:(hd + 1) * HEAD_PAD], wukt_ref[hd])


def _qlat(q_pad, wukt):
    n = q_pad.shape[0]
    return pl.pallas_call(
        _qlat_kernel,
        out_shape=jax.ShapeDtypeStruct((MLA_HEADS, n, KV_LORA), F32),
        compiler_params=pltpu.CompilerParams(vmem_limit_bytes=VMEM_LIMIT),
        name="qlat",
    )(q_pad, wukt)


def _paged_kernel(pt_ref, qlat_ref, qrope_ref, ckvn_ref, krn_ref, ckv_hbm, krt_hbm, lat_ref,
                  kvbuf, krbuf, kvb, s_scr, sem, *, n_pages):
    nb = lat_ref.shape[0]
    pc = PAGES_PER_CHUNK
    pb = PAGES_PER_BLOCK
    ns = PAGED_SLOTS
    ahead = ns - 1
    n_chunks = n_pages // pc
    page = kvbuf.shape[2]

    def copies(bb, ch, slot):
        out = []
        for p in range(pc):
            pg = pt_ref[bb, ch * pc + p]
            out.append(pltpu.make_async_copy(ckv_hbm.at[pg], kvbuf.at[slot, p], sem.at[0, slot]))
            out.append(pltpu.make_async_copy(krt_hbm.at[pg], krbuf.at[slot, p], sem.at[1, slot]))
        return out

    def start(bb, ch, slot):
        for cp in copies(bb, ch, slot):
            cp.start()

    def locate(b, ch_in_b):
        over = ch_in_b // n_chunks
        return jnp.minimum(b + over, nb - 1), ch_in_b - over * n_chunks

    def prefetch(b, ch_in_b):
        tb, tch = locate(b, ch_in_b + ahead)
        start(tb, tch, (ch_in_b + ahead) % ns)

    def load_and_score(b, ch_in_b):
        tb, tch = locate(b, ch_in_b)
        slot = ch_in_b % ns
        for cp in copies(tb, tch, slot):
            cp.wait()
        par = ch_in_b % 2
        qlat_b = qlat_ref[tb].astype(BF16)
        qrope_b = qrope_ref[tb].astype(BF16)
        scores = []
        for j in range(pc // pb):
            kv = kvbuf[slot, pl.ds(j * pb, pb)].reshape(pb * page, KV_LORA).astype(BF16)
            kvb[par, pl.ds(j * pb * page, pb * page), :] = kv
            krt = jnp.concatenate([krbuf[slot, j * pb + p] for p in range(pb)], axis=1).astype(BF16)
            scores.append(_dot_nt(qlat_b, kv) + _dot(qrope_b, krt))
        return jnp.concatenate(scores, axis=1)

    step = pl.program_id(0)
    per_step = nb // pl.num_programs(0)

    @pl.when(step == 0)
    def _():
        for ch0 in range(ahead):
            start(0, ch0, ch0)
        prefetch(0, 0)
        s_scr[...] = load_and_score(0, 0)

    def sample(u, s):
        b = step * per_step + u
        def chunk(ch, carry):
            s, m, l, acc = carry
            prefetch(b, ch + 1)
            s_next = load_and_score(b, ch + 1)
            m_new = jnp.maximum(m, jnp.max(s, axis=-1, keepdims=True))
            alpha = jnp.exp(m - m_new)
            p = jnp.exp(s - m_new)
            l = alpha * l + jnp.sum(p, axis=-1, keepdims=True)
            acc = alpha * acc + _dot(p.astype(BF16), kvb[ch % 2])
            return s_next, m_new, l, acc

        init = (s, jnp.full((MLA_HEADS, 1), NEG_BIG, F32), jnp.zeros((MLA_HEADS, 1), F32),
                jnp.zeros((MLA_HEADS, KV_LORA), F32))
        s, m, l, acc = lax.fori_loop(0, n_chunks, chunk, init)

        qlat = qlat_ref[b]
        qrope = qrope_ref[b]
        ckvn = ckvn_ref[b]
        s_new = (jnp.sum(qlat * ckvn, axis=-1, keepdims=True)
                 + jnp.sum(qrope * krn_ref[b], axis=-1, keepdims=True))
        m_new = jnp.maximum(m, s_new)
        alpha = jnp.exp(m - m_new)
        p_new = jnp.exp(s_new - m_new)
        l = alpha * l + p_new
        lat_ref[b] = (alpha * acc + p_new * ckvn) / l
        return s

    s_scr[...] = lax.fori_loop(0, per_step, sample, s_scr[...])

    @pl.when(step == pl.num_programs(0) - 1)
    def _():
        for i in range(1, ahead + 1):
            tb, tch = locate(nb - 1, n_chunks + i)
            for cp in copies(tb, tch, (n_chunks + i) % ns):
                cp.wait()


def _paged_attention(page_table, qlat, qrope, ckv_new, kr_new, cache_ckv, cache_kr):
    n, n_pages = page_table.shape
    page = cache_ckv.shape[1]
    pc = PAGES_PER_CHUNK
    whole = lambda shape: pl.BlockSpec(shape, lambda i, pt: (0, 0, 0))
    grid_spec = pltpu.PrefetchScalarGridSpec(
        num_scalar_prefetch=1,
        grid=(n // PAGED_SAMPLES_PER_STEP,),
        in_specs=[whole((n, MLA_HEADS, KV_LORA)), whole((n, MLA_HEADS, MLA_ROPE)), whole((n, 1, KV_LORA)),
                  whole((n, 1, MLA_ROPE)), pl.BlockSpec(memory_space=pl.ANY), pl.BlockSpec(memory_space=pl.ANY)],
        out_specs=whole((n, MLA_HEADS, KV_LORA)),
        scratch_shapes=[pltpu.VMEM((PAGED_SLOTS, pc, page, KV_LORA), F32),
                        pltpu.VMEM((PAGED_SLOTS, pc, MLA_ROPE, page), F32),
                        pltpu.VMEM((2, pc * page, KV_LORA), BF16),
                        pltpu.VMEM((MLA_HEADS, pc * page), F32),
                        pltpu.SemaphoreType.DMA((2, PAGED_SLOTS))],
    )
    return pl.pallas_call(
        functools.partial(_paged_kernel, n_pages=n_pages),
        grid_spec=grid_spec,
        out_shape=jax.ShapeDtypeStruct((n, MLA_HEADS, KV_LORA), F32),
        compiler_params=_params("arbitrary"),
        name="paged_mla",
    )(page_table, qlat, qrope, ckv_new, kr_new, cache_ckv, cache_kr)


def _latv_kernel(lat_ref, wuv_ref, o_ref):
    outs = [_dot(lat_ref[hd].astype(BF16), wuv_ref[:, hd * MLA_VDIM:(hd + 1) * MLA_VDIM])
            for hd in range(MLA_HEADS)]
    o_ref[...] = jnp.concatenate(outs, axis=-1).astype(o_ref.dtype)


def _latv(lat_hb, wuv):
    n = lat_hb.shape[1]
    return pl.pallas_call(
        _latv_kernel,
        out_shape=jax.ShapeDtypeStruct((n, MLA_WIDTH), BF16),
        compiler_params=pltpu.CompilerParams(vmem_limit_bytes=VMEM_LIMIT),
        name="latv",
    )(lat_hb, wuv)


HG_SAMPLE_BLOCK = 8


def _hgrn_sample_kernel(s_ref, zh_ref, lbraw_ref, gn_ref, so_ref, o_ref):
    gn = gn_ref[...]
    nbk = HG_SAMPLE_BLOCK
    pad = jnp.zeros((HG_DK - nbk, HG_DK), F32)
    heads = []
    for hd in range(HG_HEADS):
        lb = _lower_bound(lbraw_ref[:, hd * HG_DK:(hd + 1) * HG_DK])
        f_rows = lb + (1.0 - lb) * jax.nn.sigmoid(zh_ref[:, HG_QK + hd * HG_DK:HG_QK + (hd + 1) * HG_DK])
        f_cols = jnp.concatenate([f_rows, pad], axis=0).T
        q_cols = jnp.concatenate([zh_ref[:, hd * HG_DK:(hd + 1) * HG_DK], pad], axis=0).T
        outs = []
        for j in range(nbk):
            f = f_cols[:, j:j + 1]
            v = zh_ref[j:j + 1, 2 * HG_QK + hd * HG_DV:2 * HG_QK + (hd + 1) * HG_DV]
            s_new = f * s_ref[j, hd] + (1.0 - f) * v
            so_ref[j, hd] = s_new
            o = jnp.sum(q_cols[:, j:j + 1] * s_new, axis=0, keepdims=True)
            outs.append(_rms(o, gn))
        heads.append(jnp.concatenate(outs, axis=0))
    o = jnp.concatenate(heads, axis=-1)
    gate = zh_ref[:, 2 * HG_QK + HG_WIDTH:]
    o_ref[...] = (o * _silu(gate)).astype(o_ref.dtype)


def _hgrn_sample(state, zh, lb_raw, gn):
    n = state.shape[0]
    nb = HG_SAMPLE_BLOCK
    blk4 = pl.BlockSpec((nb, HG_HEADS, HG_DK, HG_DV), lambda i: (i, 0, 0, 0))
    return pl.pallas_call(
        _hgrn_sample_kernel,
        grid=(n // nb,),
        in_specs=[blk4, pl.BlockSpec((nb, zh.shape[1]), lambda i: (i, 0)),
                  _const_spec(lb_raw.shape), _const_spec((1, HG_DV))],
        out_specs=[blk4, pl.BlockSpec((nb, HG_WIDTH), lambda i: (i, 0))],
        out_shape=[jax.ShapeDtypeStruct(state.shape, F32), jax.ShapeDtypeStruct((n, HG_WIDTH), BF16)],
        compiler_params=_params("parallel"),
        name="hgrn_sample",
    )(state, zh, lb_raw, gn)


def _outproj_kernel(*refs, with_mem):
    if with_mem:
        x_ref, omla_ref, ohg_ref, wout_ref, g_ref, wmq_ref, mk_ref, mv_ref, x1_ref, qm_ref = refs
    else:
        x_ref, omla_ref, ohg_ref, wout_ref, g_ref, wmq_ref, x1_ref, qm_ref = refs
    x1 = (x_ref[...] + _dot(omla_ref[...], wout_ref[:MLA_WIDTH, :])
          + _dot(ohg_ref[...], wout_ref[MLA_WIDTH:, :]))
    x1_ref[...] = x1
    hm = _rms(x1, g_ref[...]).astype(BF16)
    qm = _dot(hm, wmq_ref[...]) * (MEM_HDIM ** -0.5)
    if not with_mem:
        qm_ref[...] = qm.astype(qm_ref.dtype)
        return
    qb = qm.astype(BF16)
    heads = [slice(hd * MEM_HDIM, (hd + 1) * MEM_HDIM) for hd in range(MEM_HEADS)]
    scores = [_dot_nt(qb[:, sl], mk_ref[:, sl]) for sl in heads]
    outs = []
    for s, sl in zip(scores, heads):
        p = jnp.exp(s - jnp.max(s, axis=-1, keepdims=True))
        o = _dot(p.astype(BF16), mv_ref[:, sl])
        outs.append(o / jnp.sum(p, axis=-1, keepdims=True))
    qm_ref[...] = jnp.concatenate(outs, axis=-1).astype(qm_ref.dtype)


def _outproj(x, omla, ohg, w, tm, q_dtype, mem=None, seq_len=None):
    rows = x.shape[0]
    row = lambda wd: pl.BlockSpec((tm, wd), lambda i: (i, 0))
    in_specs = [row(D_MODEL), row(MLA_WIDTH), row(HG_WIDTH), _const_spec((MLA_WIDTH + HG_WIDTH, D_MODEL)),
                _const_spec((1, D_MODEL)), _const_spec((D_MODEL, MEM_WIDTH))]
    args = [x, omla, ohg, w["w_out"], w["norm_memx"], w["w_mq"]]
    if mem is not None:
        tiles_per_seq = seq_len // tm
        m_tok = mem[0].shape[1]
        in_specs += [pl.BlockSpec((None, m_tok, MEM_WIDTH), lambda i: (i // tiles_per_seq, 0, 0))] * 2
        args += list(mem)
    return pl.pallas_call(
        functools.partial(_outproj_kernel, with_mem=mem is not None),
        grid=(rows // tm,),
        in_specs=in_specs,
        out_specs=[row(D_MODEL), row(MEM_WIDTH)],
        out_shape=[jax.ShapeDtypeStruct((rows, D_MODEL), F32), jax.ShapeDtypeStruct((rows, MEM_WIDTH), q_dtype)],
        compiler_params=_params("parallel"),
        name="outproj_mem" if mem is not None else "outproj",
    )(*args)


def _memattn_sample_kernel(q_ref, k_ref, v_ref, o_ref):
    nblk, rows = k_ref.shape[:2]
    hid = lax.broadcasted_iota(jnp.int32, (SUBLANE, rows), 0)
    rhd = lax.broadcasted_iota(jnp.int32, (SUBLANE, rows), 1) % MEM_HEADS
    own = hid == rhd
    zero = jnp.zeros((SUBLANE - MEM_HEADS, MEM_HDIM), F32)
    scores = []
    for j in range(nblk):
        q = jnp.concatenate([q_ref[j, :, hd * MEM_HDIM:(hd + 1) * MEM_HDIM] for hd in range(MEM_HEADS)] + [zero],
                            axis=0)
        scores.append(_dot_nt(q.astype(BF16), k_ref[j].astype(BF16)))
    for j in range(nblk):
        s = jnp.where(own, scores[j], NEG_BIG)
        p = jnp.exp(s - jnp.max(s, axis=-1, keepdims=True))
        p = jnp.where(own, p, 0.0)
        p = p / jnp.sum(p, axis=-1, keepdims=True)
        of = _dot(p.astype(BF16), v_ref[j].astype(BF16))
        o_ref[j] = jnp.concatenate([of[hd:hd + 1] for hd in range(MEM_HEADS)], axis=-1).astype(o_ref.dtype)


MEM_SAMPLE_BLOCK = 4


def _memattn_sample(q, k, v):
    n, m = k.shape[:2]
    sb = MEM_SAMPLE_BLOCK
    return pl.pallas_call(
        _memattn_sample_kernel,
        grid=(n // sb,),
        in_specs=[pl.BlockSpec((sb, 1, MEM_WIDTH), lambda i: (i, 0, 0)),
                  pl.BlockSpec((sb, m, MEM_HDIM), lambda i: (i, 0, 0)),
                  pl.BlockSpec((sb, m, MEM_HDIM), lambda i: (i, 0, 0))],
        out_specs=pl.BlockSpec((sb, 1, MEM_WIDTH), lambda i: (i, 0, 0)),
        out_shape=jax.ShapeDtypeStruct((n, 1, MEM_WIDTH), BF16),
        compiler_params=_params("parallel"),
        name="memattn_sample",
    )(q, k, v)


def _ffn_kernel(*refs, seq_mode, tiles_per_seq):
    if seq_mode:
        (x1_ref, om_ref, wmo_ref, g_ref, wup_ref, cw_ref, cb_ref, wdn_ref, gfin_ref,
         y_ref, tail_ref, carry_ref) = refs
    else:
        (x1_ref, om_ref, wmo_ref, g_ref, wup_ref, cw_ref, cb_ref, wdn_ref, gfin_ref, prev2_ref, prev1_ref,
         y_ref, tail_ref) = refs
    tm = x1_ref.shape[0]
    x2 = x1_ref[...] + _dot(om_ref[...], wmo_ref[...])
    h = _rms(x2, g_ref[...]).astype(BF16)
    acc = x2
    if seq_mode:
        first = pl.program_id(0) % tiles_per_seq == 0
        rid = lax.broadcasted_iota(jnp.int32, (tm, 1), 0)

        @pl.when(pl.program_id(0) == 0)
        def _():
            carry_ref[...] = jnp.zeros_like(carry_ref)
    ups = [(_dot(h, wup_ref[:, f0:f0 + FFN_CHUNK]), _dot(h, wup_ref[:, D_FF + f0:D_FF + f0 + FFN_CHUNK]))
           for f0 in range(0, D_FF, FFN_CHUNK)]
    for (a, gt), f0 in zip(ups, range(0, D_FF, FFN_CHUNK)):
        fs = slice(f0, f0 + FFN_CHUNK)
        if seq_mode:
            prev = jnp.where(first, 0.0, carry_ref[:, fs])
            p1 = prev[SUBLANE - 1:SUBLANE]
            p2 = prev[SUBLANE - 2:SUBLANE - 1]
            a1 = jnp.where(rid == 0, p1, pltpu.roll(a, 1, axis=0))
            a2 = jnp.where(rid == 0, p2, jnp.where(rid == 1, p1, pltpu.roll(a, 2, axis=0)))
            carry_ref[:, fs] = a[tm - SUBLANE:, :]
            tail_ref[:, fs] = a[tm - (CONV_W - 1):, :]
        else:
            a1 = prev1_ref[:, fs]
            a2 = prev2_ref[:, fs]
            tail_ref[:, fs] = a
        conv = cb_ref[:, fs] + a2 * cw_ref[0:1, fs] + a1 * cw_ref[1:2, fs] + a * cw_ref[2:3, fs]
        u = (_silu(conv) * gt).astype(BF16)
        acc = acc + _dot(u, wdn_ref[fs, :])
    y_ref[...] = _rms(acc, gfin_ref[...])


def _ffn(x1, om, w, tm, seq_len=None, prev=None):
    rows = x1.shape[0]
    seq_mode = prev is None
    row = lambda wd: pl.BlockSpec((tm, wd), lambda i: (i, 0))
    in_specs = [row(D_MODEL), row(MEM_WIDTH), _const_spec((MEM_WIDTH, D_MODEL)), _const_spec((1, D_MODEL)),
                _const_spec((D_MODEL, 2 * D_FF)), _const_spec((CONV_W, D_FF)), _const_spec((1, D_FF)),
                _const_spec((D_FF, D_MODEL)), _const_spec((1, D_MODEL))]
    args = [x1, om, w["w_mo"], w["norm_ffn"], w["w_up"], w["conv_w"], w["conv_b"], w["w_down"], w["norm_final"]]
    if seq_mode:
        tiles_per_seq = seq_len // tm
        n_seq = rows // seq_len
        tail_spec = pl.BlockSpec((None, CONV_W - 1, D_FF), lambda i: (i // tiles_per_seq, 0, 0))
        tail_shape = jax.ShapeDtypeStruct((n_seq, CONV_W - 1, D_FF), F32)
        scratch = [pltpu.VMEM((SUBLANE, D_FF), F32)]
        sem = "arbitrary"
    else:
        tiles_per_seq = 1
        in_specs += [row(D_FF), row(D_FF)]
        args += list(prev)
        tail_spec = row(D_FF)
        tail_shape = jax.ShapeDtypeStruct((rows, D_FF), F32)
        scratch = []
        sem = "parallel"
    return pl.pallas_call(
        functools.partial(_ffn_kernel, seq_mode=seq_mode, tiles_per_seq=tiles_per_seq),
        grid=(rows // tm,),
        in_specs=in_specs,
        out_specs=[row(D_MODEL), tail_spec],
        out_shape=[jax.ShapeDtypeStruct((rows, D_MODEL), F32), tail_shape],
        scratch_shapes=scratch,
        compiler_params=_params(sem),
        name="ffn_seq" if seq_mode else "ffn_step",
    )(*args)


def _rope_tables(pos):
    half = ROPE_HALF
    inv = ROPE_BASE ** (-jnp.arange(half, dtype=F32) / half)
    ang = pos[:, None] * inv[None, :]
    cos, sin = jnp.cos(ang), jnp.sin(ang)
    n = pos.shape[0]
    one = jnp.ones((n, MLA_NOPE), F32)
    zero = jnp.zeros((n, MLA_NOPE), F32)
    pad = jnp.zeros((n, LANE - MLA_NOPE - MLA_ROPE), F32)
    cos_t = jnp.concatenate([one, cos, cos, pad], axis=-1)
    sin_t = jnp.concatenate([zero, -sin, sin, pad], axis=-1)
    return cos_t, sin_t


def _rope_tile(w_rope):
    x1, x2 = w_rope[..., :ROPE_HALF], w_rope[..., ROPE_HALF:]
    return jnp.concatenate([x1, x2, x2, x1], axis=-1)


def _prep_weights(norm_mix, w_in, norm_q, norm_kv, w_uq, w_uk, w_uv, norm_hg, w_out, norm_memx, norm_mem,
                  w_mq, w_mk, w_mv, w_mo, norm_ffn, w_up, conv_w, conv_b, w_down, norm_final):
    off_kr = Q_LORA + KV_LORA
    off_hg = off_kr + MLA_ROPE
    kr_tile = jnp.concatenate([jnp.zeros((D_MODEL, MLA_NOPE), F32), _rope_tile(w_in[:, off_kr:off_hg])], axis=-1)
    w_in_pad = jnp.concatenate([w_in[:, :off_kr], kr_tile, w_in[:, off_hg:]], axis=-1)
    uq = jnp.concatenate([w_uq[..., :MLA_NOPE], _rope_tile(w_uq[..., MLA_NOPE:])], axis=-1)
    uk = jnp.concatenate([w_uk, jnp.zeros((KV_LORA, MLA_HEADS, HEAD_PAD - MLA_NOPE), F32)], axis=-1)
    ukt = jnp.transpose(uk, (1, 2, 0))
    row = lambda g: g.reshape(1, -1)
    return {
        "norm_mix": row(norm_mix), "w_in": w_in_pad.astype(BF16), "norm_q": row(norm_q), "norm_kv": row(norm_kv),
        "w_uq": uq.reshape(Q_LORA, MLA_HEADS * HEAD_PAD).astype(BF16),
        "w_uq_t": uq.reshape(Q_LORA, MLA_HEADS * HEAD_PAD).T.astype(BF16),
        "w_uv_t": w_uv.reshape(KV_LORA, MLA_WIDTH).T.astype(BF16),
        "w_uk": uk.reshape(KV_LORA, MLA_HEADS * HEAD_PAD).astype(BF16),
        "w_ukt": ukt.astype(BF16),
        "w_uv": w_uv.reshape(KV_LORA, MLA_WIDTH).astype(BF16),
        "norm_hg": row(norm_hg), "w_out": w_out.astype(BF16), "norm_memx": row(norm_memx),
        "norm_mem": row(norm_mem), "w_mq": w_mq.reshape(D_MODEL, MEM_WIDTH).astype(BF16),
        "w_mk": w_mk.reshape(D_MODEL, MEM_WIDTH).astype(BF16), "w_mv": w_mv.reshape(D_MODEL, MEM_WIDTH).astype(BF16),
        "w_mo": w_mo.reshape(MEM_WIDTH, D_MODEL).astype(BF16), "norm_ffn": row(norm_ffn),
        "w_up": w_up.astype(BF16), "conv_w": conv_w, "conv_b": row(conv_b), "w_down": w_down.astype(BF16),
        "norm_final": row(norm_final),
    }


def kernel(x_prompt, x_sample, mem_prompt, cache_ckv, cache_krope, page_table, cache_mem_k, cache_mem_v,
           state_hgrn, state_conv, norm_mix, w_in, norm_q, norm_kv, w_uq, w_uk, w_uv, hg_lb_raw, norm_hg,
           w_out, norm_memx, norm_mem, w_mq, w_mk, w_mv, w_mo, norm_ffn, w_up, conv_w, conv_b, w_down,
           norm_final):
    bp, t, _ = x_prompt.shape
    db = x_sample.shape[0]
    n_pages = page_table.shape[1]
    page = cache_ckv.shape[2]
    past_len = n_pages * page
    w = _prep_weights(norm_mix[0], w_in[0], norm_q[0], norm_kv[0], w_uq[0], w_uk[0], w_uv[0], norm_hg[0],
                      w_out[0], norm_memx[0], norm_mem[0], w_mq[0], w_mk[0], w_mv[0], w_mo[0], norm_ffn[0],
                      w_up[0], conv_w[0], conv_b[0], w_down[0], norm_final)
    tm = ROW_TILE
    rows_p = bp * t

    mk_p, mv_p, mk_pb, mv_pb = _memkv(mem_prompt.reshape(-1, D_MODEL), w["norm_mem"], w["w_mk"], w["w_mv"])
    cos_p, sin_p = _rope_tables(jnp.arange(t, dtype=F32))
    xp = x_prompt.reshape(rows_p, D_MODEL)
    qt_p, k_p, vt_p, ckv_p, kr_p, zh_p = _mix_in(xp, w, cos_p, sin_p, tm, True)
    hq = MLA_HEADS * HEAD_PAD
    o_mla_p = _mla_prompt(qt_p, k_p.reshape(bp, t, hq), vt_p, bp, t, tm)
    o_hg_p, s_p = _hgrn_prompt(zh_p.reshape(bp, t, -1), hg_lb_raw, w["norm_hg"])
    m_tok = mem_prompt.shape[1]
    x1_p, om_p = _outproj(xp, o_mla_p.reshape(rows_p, -1), o_hg_p.reshape(rows_p, -1), w, tm, BF16,
                          mem=(mk_pb.reshape(bp, m_tok, MEM_WIDTH), mv_pb.reshape(bp, m_tok, MEM_WIDTH)), seq_len=t)
    y_p, cv_p = _ffn(x1_p, om_p, w, tm, seq_len=t)

    cos_s, sin_s = _rope_tables(jnp.full((db,), past_len, F32))
    xs = x_sample.reshape(db, D_MODEL)
    q_s, _, _, ckv_s, kr_s, zh_s = _mix_in(xs, w, cos_s, sin_s, db, False)
    qlat = jnp.transpose(_qlat(q_s, w["w_ukt"]), (1, 0, 2))
    qrope = q_s.reshape(db, MLA_HEADS, HEAD_PAD)[:, :, MLA_NOPE:MLA_NOPE + MLA_ROPE].astype(F32)
    kr_new = kr_s[:, MLA_NOPE:MLA_NOPE + MLA_ROPE]
    lat = _paged_attention(page_table, qlat, qrope, ckv_s.reshape(db, 1, KV_LORA), kr_new.reshape(db, 1, MLA_ROPE),
                           cache_ckv.reshape(-1, page, KV_LORA),
                           jnp.swapaxes(cache_krope, 2, 3).reshape(-1, MLA_ROPE, page))
    o_mla_s = _latv(jnp.transpose(lat, (1, 0, 2)), w["w_uv"])
    s_s, o_hg_s = _hgrn_sample(state_hgrn.reshape(db, HG_HEADS, HG_DK, HG_DV), zh_s, hg_lb_raw, w["norm_hg"])
    x1_s, qm_s = _outproj(xs, o_mla_s, o_hg_s, w, db, F32)
    om_s = _memattn_sample(qm_s.reshape(db, 1, MEM_WIDTH), cache_mem_k.reshape(db, m_tok * MEM_HEADS, MEM_HDIM),
                           cache_mem_v.reshape(db, m_tok * MEM_HEADS, MEM_HDIM))
    y_s, a_s = _ffn(x1_s, om_s.reshape(db, MEM_WIDTH), w, db, prev=(state_conv[0, :, 0], state_conv[0, :, 1]))
    cv_s = jnp.stack([state_conv[0, :, 1], a_s], axis=1)

    rope_sl = slice(MLA_NOPE, MLA_NOPE + MLA_ROPE)
    return (y_p.reshape(bp, t, D_MODEL), y_s.reshape(db, 1, D_MODEL),
            ckv_p.reshape(1, bp, t, KV_LORA), kr_p[:, rope_sl].reshape(1, bp, t, MLA_ROPE),
            mk_p.reshape(1, bp, m_tok, MEM_HEADS, MEM_HDIM), mv_p.reshape(1, bp, m_tok, MEM_HEADS, MEM_HDIM),
            s_p[None], cv_p[None],
            ckv_s.reshape(1, db, 1, KV_LORA), kr_new.reshape(1, db, 1, MLA_ROPE),
            s_s[None], cv_s[None])
```

```python
import functools

import jax
import jax.numpy as jnp
from jax import lax
from jax.experimental import pallas as pl
from jax.experimental.pallas import tpu as pltpu

F32 = jnp.float32
BF16 = jnp.bfloat16

D_MODEL = 1024
MLA_HEADS = 8
MLA_NOPE = 64
MLA_ROPE = 32
MLA_VDIM = 64
Q_LORA = 384
KV_LORA = 256
HG_HEADS = 4
HG_DK = 128
HG_DV = 128
HG_CHUNK = 32
HG_QK = HG_HEADS * HG_DK
HG_WIDTH = HG_HEADS * HG_DV
MLA_WIDTH = MLA_HEADS * MLA_VDIM
MEM_HEADS = 4
MEM_HDIM = 128
MEM_WIDTH = MEM_HEADS * MEM_HDIM
D_FF = 2816
CONV_W = 3
ROPE_BASE = 10000.0
EPS = 1e-6

LANE = 128
SUBLANE = 8
HEAD_PAD = LANE
ROPE_HALF = MLA_ROPE // 2
OFF_CKV = Q_LORA
OFF_KRT = OFF_CKV + KV_LORA
OFF_HGP = OFF_KRT + LANE
IN_COLS_PAD = OFF_HGP + 2 * HG_QK + 2 * HG_WIDTH
VMEM_LIMIT = 56 * 1024 * 1024
ROW_TILE = 512
FFN_CHUNK = D_FF // 2
PAGES_PER_CHUNK = 16
PAGES_PER_BLOCK = 8
PAGED_SLOTS = 8
PAGED_SAMPLES_PER_STEP = 16
HG_GROUP = 256
NEG_BIG = -1e30
LOG2_E = 1.4426950408889634


def _rms(x, g):
    return x * lax.rsqrt(jnp.mean(x * x, axis=-1, keepdims=True) + EPS) * g


def _dot(a, b):
    return jnp.dot(a, b, preferred_element_type=F32)


def _dot_nt(a, b):
    return lax.dot_general(a, b, (((1,), (1,)), ((), ())), preferred_element_type=F32)


def _dot_tn(a, b):
    return lax.dot_general(a, b, (((0,), (0,)), ((), ())), preferred_element_type=F32)


def _silu(x):
    return x * jax.nn.sigmoid(x)


def _params(*sem):
    return pltpu.CompilerParams(dimension_semantics=sem, vmem_limit_bytes=VMEM_LIMIT)


def _const_spec(shape):
    nd = len(shape)
    return pl.BlockSpec(shape, lambda *_: (0,) * nd, pipeline_mode=pl.Buffered(1))


def _memkv_kernel(mem_ref, g_ref, wk_ref, wv_ref, k_ref, v_ref, kb_ref, vb_ref):
    m = _rms(mem_ref[...], g_ref[...]).astype(BF16)
    k = _dot(m, wk_ref[...])
    v = _dot(m, wv_ref[...])
    k_ref[...] = k
    v_ref[...] = v
    kb_ref[...] = k.astype(BF16)
    vb_ref[...] = v.astype(BF16)


def _memkv(mem, g, wk, wv):
    rows = mem.shape[0]
    tm = min(ROW_TILE, rows)
    row = lambda w: pl.BlockSpec((tm, w), lambda i: (i, 0))
    return pl.pallas_call(
        _memkv_kernel,
        grid=(rows // tm,),
        in_specs=[row(D_MODEL), _const_spec((1, D_MODEL)), _const_spec((D_MODEL, MEM_WIDTH)),
                  _const_spec((D_MODEL, MEM_WIDTH))],
        out_specs=[row(MEM_WIDTH)] * 4,
        out_shape=[jax.ShapeDtypeStruct((rows, MEM_WIDTH), F32)] * 2
        + [jax.ShapeDtypeStruct((rows, MEM_WIDTH), BF16)] * 2,
        compiler_params=_params("parallel"),
        name="memkv",
    )(mem, g, wk, wv)


def _mix_in_kernel(*refs, transposed):
    if transposed:
        (x_ref, g_ref, win_ref, gq_ref, gkv_ref, wuq_ref, wuk_ref, wuv_ref, cos_ref, sin_ref, cost_ref, sint_ref,
         q_ref, k_ref, v_ref, ckv_ref, kr_ref, zh_ref) = refs
    else:
        (x_ref, g_ref, win_ref, gq_ref, gkv_ref, wuq_ref, wuk_ref, wuv_ref, cos_ref, sin_ref,
         q_ref, k_ref, v_ref, ckv_ref, kr_ref, zh_ref) = refs
    h = _rms(x_ref[...], g_ref[...]).astype(BF16)
    z = _dot(h, win_ref[...])
    c_q = _rms(z[:, :OFF_CKV], gq_ref[...]).astype(BF16)
    c_kv = _rms(z[:, OFF_CKV:OFF_KRT], gkv_ref[...])
    ckv_ref[...] = c_kv
    c_kvb = c_kv.astype(BF16)
    zh_ref[...] = z[:, OFF_HGP:]
    cos = cos_ref[...]
    sin = sin_ref[...]
    zkr = z[:, OFF_KRT:OFF_HGP]
    kr = zkr * cos + pltpu.roll(zkr, LANE - MLA_ROPE, axis=1) * sin
    kr_ref[...] = kr
    scale = (MLA_NOPE + MLA_ROPE) ** -0.5
    kn = _dot(c_kvb, wuk_ref[...])
    for hd in range(MLA_HEADS):
        sl = slice(hd * HEAD_PAD, (hd + 1) * HEAD_PAD)
        k_ref[:, sl] = (kn[:, sl] + kr).astype(BF16)
    if transposed:
        cos_q = cost_ref[...] * (scale * LOG2_E)
        sin_q = sint_ref[...] * (scale * LOG2_E)
        qt = _dot_nt(wuq_ref[...], c_q)
        for hd in range(MLA_HEADS):
            sl = slice(hd * HEAD_PAD, (hd + 1) * HEAD_PAD)
            qh = qt[sl, :]
            q_ref[sl, :] = (qh * cos_q + pltpu.roll(qh, LANE - MLA_ROPE, axis=0) * sin_q).astype(BF16)
        v_ref[...] = _dot_nt(wuv_ref[...], c_kvb).astype(BF16)
    else:
        cos_q = cos * scale
        sin_q = sin * scale
        q = _dot(c_q, wuq_ref[...])
        for hd in range(MLA_HEADS):
            sl = slice(hd * HEAD_PAD, (hd + 1) * HEAD_PAD)
            qh = q[:, sl]
            q_ref[:, sl] = (qh * cos_q + pltpu.roll(qh, LANE - MLA_ROPE, axis=1) * sin_q).astype(BF16)
        v_ref[...] = _dot(c_kvb, wuv_ref[...]).astype(BF16)


def _mix_in(x, w, cos_t, sin_t, tm, transposed):
    rows = x.shape[0]
    t_tiles = cos_t.shape[0] // tm
    row = lambda wd: pl.BlockSpec((tm, wd), lambda i: (i, 0))
    tab = pl.BlockSpec((tm, LANE), lambda i: (i % t_tiles, 0))
    hq = MLA_HEADS * HEAD_PAD
    in_specs = [row(D_MODEL), _const_spec((1, D_MODEL)), _const_spec((D_MODEL, IN_COLS_PAD)),
                _const_spec((1, Q_LORA)), _const_spec((1, KV_LORA))]
    args = [x, w["norm_mix"], w["w_in"], w["norm_q"], w["norm_kv"]]
    if transposed:
        tile_t = lambda wd: pl.BlockSpec((None, wd, tm), lambda i: (i, 0, 0))
        tab_t = pl.BlockSpec((LANE, tm), lambda i: (0, i % t_tiles))
        in_specs += [_const_spec((hq, Q_LORA)), _const_spec((KV_LORA, hq)), _const_spec((MLA_WIDTH, KV_LORA)),
                     tab, tab, tab_t, tab_t]
        args += [w["w_uq_t"], w["w_uk"], w["w_uv_t"], cos_t, sin_t, cos_t.T, sin_t.T]
        qv_specs = [tile_t(hq), row(hq), tile_t(MLA_WIDTH)]
        qv_shapes = [jax.ShapeDtypeStruct((rows // tm, hq, tm), BF16), jax.ShapeDtypeStruct((rows, hq), BF16),
                     jax.ShapeDtypeStruct((rows // tm, MLA_WIDTH, tm), BF16)]
    else:
        in_specs += [_const_spec((Q_LORA, hq)), _const_spec((KV_LORA, hq)), _const_spec((KV_LORA, MLA_WIDTH)),
                     tab, tab]
        args += [w["w_uq"], w["w_uk"], w["w_uv"], cos_t, sin_t]
        qv_specs = [row(hq), row(hq), row(MLA_WIDTH)]
        qv_shapes = [jax.ShapeDtypeStruct((rows, hq), BF16), jax.ShapeDtypeStruct((rows, hq), BF16),
                     jax.ShapeDtypeStruct((rows, MLA_WIDTH), BF16)]
    return pl.pallas_call(
        functools.partial(_mix_in_kernel, transposed=transposed),
        grid=(rows // tm,),
        in_specs=in_specs,
        out_specs=qv_specs + [row(KV_LORA), row(LANE), row(4 * HG_QK)],
        out_shape=qv_shapes + [jax.ShapeDtypeStruct((rows, KV_LORA), F32),
                               jax.ShapeDtypeStruct((rows, LANE), F32), jax.ShapeDtypeStruct((rows, 4 * HG_QK), F32)],
        compiler_params=_params("parallel"),
        name="mix_in_t" if transposed else "mix_in",
    )(*args)


HEADS_PER_STEP = 2
ATTN_KEY_TILE = 128
ATTN_QUERY_TILE = 256
ATTN_LOOKAHEAD = 12
ATTN_ONES_ROWS = 16


def _mla_prompt_kernel(qt_ref, k_ref, vt_ref, o_ref, *, bq):
    kt_n = bq // ATTN_KEY_TILE
    qh_n = bq // ATTN_QUERY_TILE
    chains = [(hh, qh) for hh in range(HEADS_PER_STEP) for qh in range(qh_n)]
    qw = ATTN_QUERY_TILE
    ones = jnp.ones((ATTN_ONES_ROWS, ATTN_KEY_TILE), BF16)

    def block(j, carry, qi, masked):
        carry = list(carry)
        ops = [(kt * ATTN_KEY_TILE, ci) for kt in range(kt_n) for ci in range(len(chains))
               if not (masked and kt * ATTN_KEY_TILE > chains[ci][1] * ATTN_QUERY_TILE + ATTN_QUERY_TILE - 1)]

        def scores(k0, ci):
            hh, qh = chains[ci]
            q0 = qh * ATTN_QUERY_TILE
            ksl = slice(hh * HEAD_PAD, (hh + 1) * HEAD_PAD)
            rows = pl.ds(pl.multiple_of(j * bq + k0, ATTN_KEY_TILE), ATTN_KEY_TILE)
            st = _dot(k_ref[rows, ksl], qt_ref[qi, ksl, q0:q0 + ATTN_QUERY_TILE])
            if masked and k0 + ATTN_KEY_TILE - 1 > q0:
                r = lax.broadcasted_iota(jnp.int32, st.shape, 0) + k0
                c = lax.broadcasted_iota(jnp.int32, st.shape, 1) + q0
                st = jnp.where(r <= c, st, NEG_BIG)
            return st

        pending = [scores(*op) for op in ops[:ATTN_LOOKAHEAD]]
        for i, (k0, ci) in enumerate(ops):
            if i + ATTN_LOOKAHEAD < len(ops):
                pending.append(scores(*ops[i + ATTN_LOOKAHEAD]))
            st = pending.pop(0)
            m, acc = carry[ci]
            hh = chains[ci][0]
            vsl = slice(hh * MLA_VDIM, (hh + 1) * MLA_VDIM)
            m_new = jnp.maximum(m, jnp.max(st, axis=0, keepdims=True))
            alpha = jnp.exp2(m - m_new)
            pt = jnp.exp2(st - m_new)
            vt_ext = jnp.concatenate([vt_ref[j, vsl, k0:k0 + ATTN_KEY_TILE], ones], axis=0)
            acc = alpha * acc + _dot(vt_ext, pt.astype(BF16))
            carry[ci] = (m_new, acc)
        return tuple(carry)

    def query_block(qi, _):
        init = tuple((jnp.full((1, qw), NEG_BIG, F32), jnp.zeros((MLA_VDIM + ATTN_ONES_ROWS, qw), F32))
                     for _ in chains)
        carry = lax.fori_loop(0, qi, functools.partial(block, qi=qi, masked=False), init)
        carry = block(qi, carry, qi, True)
        outs = [acc[:MLA_VDIM] / acc[MLA_VDIM:MLA_VDIM + 1] for _, acc in carry]
        heads = [jnp.concatenate(outs[hh * qh_n:(hh + 1) * qh_n], axis=1) for hh in range(HEADS_PER_STEP)]
        o_ref[pl.ds(pl.multiple_of(qi * bq, bq), bq), :] = jnp.concatenate(heads, axis=0).T.astype(o_ref.dtype)
        return 0

    lax.fori_loop(0, qt_ref.shape[0], query_block, 0)


def _mla_prompt(qt, k, vt, b, t, bq):
    hp = HEADS_PER_STEP
    nq = t // bq
    return pl.pallas_call(
        functools.partial(_mla_prompt_kernel, bq=bq),
        grid=(b, MLA_HEADS // hp),
        in_specs=[pl.BlockSpec((nq, hp * HEAD_PAD, bq), lambda bi, hi: (bi, hi, 0)),
                  pl.BlockSpec((None, t, hp * HEAD_PAD), lambda bi, hi: (bi, 0, hi)),
                  pl.BlockSpec((nq, hp * MLA_VDIM, bq), lambda bi, hi: (bi, hi, 0))],
        out_specs=pl.BlockSpec((None, t, hp * MLA_VDIM), lambda bi, hi: (bi, 0, hi)),
        out_shape=jax.ShapeDtypeStruct((b, t, MLA_WIDTH), BF16),
        compiler_params=_params("parallel", "parallel"),
        name="mla_prompt",
    )(qt, k, vt)


def _lower_bound(raw):
    e = jnp.exp(raw - jnp.max(raw, axis=0, keepdims=True))
    return e[0:1] / jnp.sum(e, axis=0, keepdims=True)


def _hgrn_prompt_kernel(q_ref, f_ref, i_ref, gate_ref, lbraw_ref, gn_ref, o_ref, s_ref,
                        qt_scr, o_scr, b_scr, u_scr, *, t):
    c = HG_CHUNK
    g = HG_GROUP
    cpg = g // c
    lb = _lower_bound(lbraw_ref[...])
    pos = lax.broadcasted_iota(jnp.int32, (g, HG_DK), 0) % c
    r2 = lax.broadcasted_iota(jnp.int32, (g, g), 0)
    c2 = lax.broadcasted_iota(jnp.int32, (g, g), 1)
    amask = jnp.logical_and(c2 <= r2, c2 >= r2 - r2 % c)

    def group(gi, carry):
        rows = pl.ds(pl.multiple_of(gi * g, g), g)
        q = q_ref[rows, :]
        vb = i_ref[rows, :].astype(BF16)
        f = lb + (1.0 - lb) * jax.nn.sigmoid(f_ref[rows, :])
        k = 1.0 - f
        bcum = jnp.log(f)
        for sh in (1, 2, 4, 8, 16):
            bcum = bcum + jnp.where(pos >= sh, pltpu.roll(bcum, sh, axis=0), 0.0)
        b_scr[rows, :] = bcum
        q_t = (q * jnp.exp(bcum)).astype(BF16)
        k_t = (k * jnp.exp(-bcum)).astype(BF16)
        qt_scr[rows, :] = q_t
        a = _dot_nt(q_t, k_t)
        b3 = bcum.reshape(cpg, c, HG_DK)
        k_s = (k.reshape(cpg, c, HG_DK) * jnp.exp(b3[:, c - 1:c, :] - b3)).astype(BF16)
        v3 = vb.reshape(cpg, c, HG_DV)
        for ci in range(cpg):
            u_scr[gi * cpg + ci] = _dot_tn(v3[ci], k_s[ci])
        o_scr[rows, :] = _dot(jnp.where(amask, a, 0.0).astype(BF16), vb)
        return carry

    lax.fori_loop(0, t // g, group, 0, unroll=4)

    def step(ci, st):
        rows = pl.ds(pl.multiple_of(ci * c, c), c)
        o_scr[rows, :] = o_scr[rows, :] + _dot_nt(qt_scr[rows, :], st.astype(BF16))
        decay = jnp.exp(b_scr[pl.ds(ci * c + c - 1, 1), :])
        return st * decay + u_scr[ci]

    st = lax.fori_loop(0, t // c, step, jnp.zeros((HG_DV, HG_DK), F32), unroll=32)
    s_ref[...] = st.T

    gn = gn_ref[...]

    def finish(gi, carry):
        rows = pl.ds(pl.multiple_of(gi * g, g), g)
        o_ref[rows, :] = (_rms(o_scr[rows, :], gn) * _silu(gate_ref[rows, :])).astype(o_ref.dtype)
        return carry

    lax.fori_loop(0, t // g, finish, 0, unroll=4)


def _hgrn_prompt(zh, lb_raw, gn):
    b, t, _ = zh.shape
    col = lambda off: pl.BlockSpec((None, t, HG_DK), lambda bi, hi: (bi, 0, off + hi))
    return pl.pallas_call(
        functools.partial(_hgrn_prompt_kernel, t=t),
        grid=(b, HG_HEADS),
        in_specs=[col(0), col(HG_HEADS), col(2 * HG_HEADS), col(3 * HG_HEADS),
                  pl.BlockSpec((lb_raw.shape[0], HG_DK), lambda bi, hi: (0, hi)), _const_spec((1, HG_DV))],
        out_specs=[pl.BlockSpec((None, t, HG_DV), lambda bi, hi: (bi, 0, hi)),
                   pl.BlockSpec((None, None, HG_DK, HG_DV), lambda bi, hi: (bi, hi, 0, 0))],
        out_shape=[jax.ShapeDtypeStruct((b, t, HG_WIDTH), BF16),
                   jax.ShapeDtypeStruct((b, HG_HEADS, HG_DK, HG_DV), F32)],
        scratch_shapes=[pltpu.VMEM((t, HG_DK), BF16), pltpu.VMEM((t, HG_DV), F32), pltpu.VMEM((t, HG_DK), F32),
                        pltpu.VMEM((t // HG_CHUNK, HG_DV, HG_DK), F32)],
        compiler_params=_params("parallel", "parallel"),
        name="hgrn_prompt",
    )(zh, zh, zh, zh, lb_raw, gn)


def _qlat_kernel(q_ref, wukt_ref, o_ref):
    for hd in range(MLA_HEADS):
        o_ref[hd] = _dot(q_ref[:, hd * HEAD_PAD:(hd + 1) * HEAD_PAD], wukt_ref[hd])


def _qlat(q_pad, wukt):
    n = q_pad.shape[0]
    return pl.pallas_call(
        _qlat_kernel,
        out_shape=jax.ShapeDtypeStruct((MLA_HEADS, n, KV_LORA), F32),
        compiler_params=pltpu.CompilerParams(vmem_limit_bytes=VMEM_LIMIT),
        name="qlat",
    )(q_pad, wukt)


def _paged_kernel(pt_ref, qlat_ref, qrope_ref, ckvn_ref, krn_ref, ckv_hbm, krt_hbm, lat_ref,
                  kvbuf, krbuf, kvb, s_scr, sem, *, n_pages):
    nb = lat_ref.shape[0]
    pc = PAGES_PER_CHUNK
    pb = PAGES_PER_BLOCK
    ns = PAGED_SLOTS
    ahead = ns - 1
    n_chunks = n_pages // pc
    page = kvbuf.shape[2]

    def copies(bb, ch, slot):
        out = []
        for p in range(pc):
            pg = pt_ref[bb, ch * pc + p]
            out.append(pltpu.make_async_copy(ckv_hbm.at[pg], kvbuf.at[slot, p], sem.at[0, slot]))
            out.append(pltpu.make_async_copy(krt_hbm.at[pg], krbuf.at[slot, p], sem.at[1, slot]))
        return out

    def start(bb, ch, slot):
        for i, cp in enumerate(copies(bb, ch, slot)):
            cp.start(priority=(i // 2) % 2)

    def locate(b, ch_in_b):
        over = ch_in_b // n_chunks
        return jnp.minimum(b + over, nb - 1), ch_in_b - over * n_chunks

    def prefetch(b, ch_in_b):
        tb, tch = locate(b, ch_in_b + ahead)
        start(tb, tch, (ch_in_b + ahead) % ns)

    def load_and_score(b, ch_in_b):
        tb, tch = locate(b, ch_in_b)
        slot = ch_in_b % ns
        for cp in copies(tb, tch, slot):
            cp.wait()
        par = ch_in_b % 2
        qlat_b = qlat_ref[tb].astype(BF16)
        qrope_b = qrope_ref[tb].astype(BF16)
        scores = []
        for j in range(pc // pb):
            kv = kvbuf[slot, pl.ds(j * pb, pb)].reshape(pb * page, KV_LORA).astype(BF16)
            kvb[par, pl.ds(j * pb * page, pb * page), :] = kv
            krt = jnp.concatenate([krbuf[slot, j * pb + p] for p in range(pb)], axis=1).astype(BF16)
            scores.append(_dot_nt(qlat_b, kv) + _dot(qrope_b, krt))
        return jnp.concatenate(scores, axis=1)

    step = pl.program_id(0)
    per_step = nb // pl.num_programs(0)

    @pl.when(step == 0)
    def _():
        for ch0 in range(ahead):
            start(0, ch0, ch0)
        prefetch(0, 0)
        s_scr[...] = load_and_score(0, 0)

    def sample(u, s):
        b = step * per_step + u
        def chunk(ch, carry):
            s, m, l, acc = carry
            prefetch(b, ch + 1)
            s_next = load_and_score(b, ch + 1)
            m_new = jnp.maximum(m, jnp.max(s, axis=-1, keepdims=True))
            alpha = jnp.exp(m - m_new)
            p = jnp.exp(s - m_new)
            l = alpha * l + jnp.sum(p, axis=-1, keepdims=True)
            acc = alpha * acc + _dot(p.astype(BF16), kvb[ch % 2])
            return s_next, m_new, l, acc

        init = (s, jnp.full((MLA_HEADS, 1), NEG_BIG, F32), jnp.zeros((MLA_HEADS, 1), F32),
                jnp.zeros((MLA_HEADS, KV_LORA), F32))
        s, m, l, acc = lax.fori_loop(0, n_chunks, chunk, init)

        qlat = qlat_ref[b]
        qrope = qrope_ref[b]
        ckvn = ckvn_ref[b]
        s_new = (jnp.sum(qlat * ckvn, axis=-1, keepdims=True)
                 + jnp.sum(qrope * krn_ref[b], axis=-1, keepdims=True))
        m_new = jnp.maximum(m, s_new)
        alpha = jnp.exp(m - m_new)
        p_new = jnp.exp(s_new - m_new)
        l = alpha * l + p_new
        lat_ref[b] = (alpha * acc + p_new * ckvn) / l
        return s

    s_scr[...] = lax.fori_loop(0, per_step, sample, s_scr[...])

    @pl.when(step == pl.num_programs(0) - 1)
    def _():
        for i in range(1, ahead + 1):
            tb, tch = locate(nb - 1, n_chunks + i)
            for cp in copies(tb, tch, (n_chunks + i) % ns):
                cp.wait()


def _paged_attention(page_table, qlat, qrope, ckv_new, kr_new, cache_ckv, cache_kr):
    n, n_pages = page_table.shape
    page = cache_ckv.shape[1]
    pc = PAGES_PER_CHUNK
    whole = lambda shape: pl.BlockSpec(shape, lambda i, pt: (0, 0, 0))
    grid_spec = pltpu.PrefetchScalarGridSpec(
        num_scalar_prefetch=1,
        grid=(n // PAGED_SAMPLES_PER_STEP,),
        in_specs=[whole((n, MLA_HEADS, KV_LORA)), whole((n, MLA_HEADS, MLA_ROPE)), whole((n, 1, KV_LORA)),
                  whole((n, 1, MLA_ROPE)), pl.BlockSpec(memory_space=pl.ANY), pl.BlockSpec(memory_space=pl.ANY)],
        out_specs=whole((n, MLA_HEADS, KV_LORA)),
        scratch_shapes=[pltpu.VMEM((PAGED_SLOTS, pc, page, KV_LORA), F32),
                        pltpu.VMEM((PAGED_SLOTS, pc, MLA_ROPE, page), F32),
                        pltpu.VMEM((2, pc * page, KV_LORA), BF16),
                        pltpu.VMEM((MLA_HEADS, pc * page), F32),
                        pltpu.SemaphoreType.DMA((2, PAGED_SLOTS))],
    )
    return pl.pallas_call(
        functools.partial(_paged_kernel, n_pages=n_pages),
        grid_spec=grid_spec,
        out_shape=jax.ShapeDtypeStruct((n, MLA_HEADS, KV_LORA), F32),
        compiler_params=_params("arbitrary"),
        name="paged_mla",
    )(page_table, qlat, qrope, ckv_new, kr_new, cache_ckv, cache_kr)


def _latv_kernel(lat_ref, wuv_ref, o_ref):
    outs = [_dot(lat_ref[hd].astype(BF16), wuv_ref[:, hd * MLA_VDIM:(hd + 1) * MLA_VDIM])
            for hd in range(MLA_HEADS)]
    o_ref[...] = jnp.concatenate(outs, axis=-1).astype(o_ref.dtype)


def _latv(lat_hb, wuv):
    n = lat_hb.shape[1]
    return pl.pallas_call(
        _latv_kernel,
        out_shape=jax.ShapeDtypeStruct((n, MLA_WIDTH), BF16),
        compiler_params=pltpu.CompilerParams(vmem_limit_bytes=VMEM_LIMIT),
        name="latv",
    )(lat_hb, wuv)


HG_SAMPLE_BLOCK = 8


def _hgrn_sample_kernel(s_ref, zh_ref, lbraw_ref, gn_ref, so_ref, o_ref):
    gn = gn_ref[...]
    nbk = HG_SAMPLE_BLOCK
    pad = jnp.zeros((HG_DK - nbk, HG_DK), F32)
    heads = []
    for hd in range(HG_HEADS):
        lb = _lower_bound(lbraw_ref[:, hd * HG_DK:(hd + 1) * HG_DK])
        f_rows = lb + (1.0 - lb) * jax.nn.sigmoid(zh_ref[:, HG_QK + hd * HG_DK:HG_QK + (hd + 1) * HG_DK])
        f_cols = jnp.concatenate([f_rows, pad], axis=0).T
        q_cols = jnp.concatenate([zh_ref[:, hd * HG_DK:(hd + 1) * HG_DK], pad], axis=0).T
        outs = []
        for j in range(nbk):
            f = f_cols[:, j:j + 1]
            v = zh_ref[j:j + 1, 2 * HG_QK + hd * HG_DV:2 * HG_QK + (hd + 1) * HG_DV]
            s_new = f * s_ref[j, hd] + (1.0 - f) * v
            so_ref[j, hd] = s_new
            o = jnp.sum(q_cols[:, j:j + 1] * s_new, axis=0, keepdims=True)
            outs.append(_rms(o, gn))
        heads.append(jnp.concatenate(outs, axis=0))
    o = jnp.concatenate(heads, axis=-1)
    gate = zh_ref[:, 2 * HG_QK + HG_WIDTH:]
    o_ref[...] = (o * _silu(gate)).astype(o_ref.dtype)


def _hgrn_sample(state, zh, lb_raw, gn):
    n = state.shape[0]
    nb = HG_SAMPLE_BLOCK
    blk4 = pl.BlockSpec((nb, HG_HEADS, HG_DK, HG_DV), lambda i: (i, 0, 0, 0))
    return pl.pallas_call(
        _hgrn_sample_kernel,
        grid=(n // nb,),
        in_specs=[blk4, pl.BlockSpec((nb, zh.shape[1]), lambda i: (i, 0)),
                  _const_spec(lb_raw.shape), _const_spec((1, HG_DV))],
        out_specs=[blk4, pl.BlockSpec((nb, HG_WIDTH), lambda i: (i, 0))],
        out_shape=[jax.ShapeDtypeStruct(state.shape, F32), jax.ShapeDtypeStruct((n, HG_WIDTH), BF16)],
        compiler_params=_params("parallel"),
        name="hgrn_sample",
    )(state, zh, lb_raw, gn)


def _outproj_kernel(*refs, with_mem):
    if with_mem:
        x_ref, omla_ref, ohg_ref, wout_ref, g_ref, wmq_ref, mk_ref, mv_ref, x1_ref, qm_ref = refs
    else:
        x_ref, omla_ref, ohg_ref, wout_ref, g_ref, wmq_ref, x1_ref, qm_ref = refs
    x1 = (x_ref[...] + _dot(omla_ref[...], wout_ref[:MLA_WIDTH, :])
          + _dot(ohg_ref[...], wout_ref[MLA_WIDTH:, :]))
    x1_ref[...] = x1
    hm = _rms(x1, g_ref[...]).astype(BF16)
    qm = _dot(hm, wmq_ref[...]) * (MEM_HDIM ** -0.5)
    if not with_mem:
        qm_ref[...] = qm.astype(qm_ref.dtype)
        return
    qb = qm.astype(BF16)
    heads = [slice(hd * MEM_HDIM, (hd + 1) * MEM_HDIM) for hd in range(MEM_HEADS)]
    scores = [_dot_nt(qb[:, sl], mk_ref[:, sl]) for sl in heads]
    outs = []
    for s, sl in zip(scores, heads):
        p = jnp.exp(s - jnp.max(s, axis=-1, keepdims=True))
        o = _dot(p.astype(BF16), mv_ref[:, sl])
        outs.append(o / jnp.sum(p, axis=-1, keepdims=True))
    qm_ref[...] = jnp.concatenate(outs, axis=-1).astype(qm_ref.dtype)


def _outproj(x, omla, ohg, w, tm, q_dtype, mem=None, seq_len=None):
    rows = x.shape[0]
    row = lambda wd: pl.BlockSpec((tm, wd), lambda i: (i, 0))
    in_specs = [row(D_MODEL), row(MLA_WIDTH), row(HG_WIDTH), _const_spec((MLA_WIDTH + HG_WIDTH, D_MODEL)),
                _const_spec((1, D_MODEL)), _const_spec((D_MODEL, MEM_WIDTH))]
    args = [x, omla, ohg, w["w_out"], w["norm_memx"], w["w_mq"]]
    if mem is not None:
        tiles_per_seq = seq_len // tm
        m_tok = mem[0].shape[1]
        in_specs += [pl.BlockSpec((None, m_tok, MEM_WIDTH), lambda i: (i // tiles_per_seq, 0, 0))] * 2
        args += list(mem)
    return pl.pallas_call(
        functools.partial(_outproj_kernel, with_mem=mem is not None),
        grid=(rows // tm,),
        in_specs=in_specs,
        out_specs=[row(D_MODEL), row(MEM_WIDTH)],
        out_shape=[jax.ShapeDtypeStruct((rows, D_MODEL), F32), jax.ShapeDtypeStruct((rows, MEM_WIDTH), q_dtype)],
        compiler_params=_params("parallel"),
        name="outproj_mem" if mem is not None else "outproj",
    )(*args)


def _memattn_sample_kernel(q_ref, k_ref, v_ref, o_ref):
    nblk, rows = k_ref.shape[:2]
    hid = lax.broadcasted_iota(jnp.int32, (SUBLANE, rows), 0)
    rhd = lax.broadcasted_iota(jnp.int32, (SUBLANE, rows), 1) % MEM_HEADS
    own = hid == rhd
    zero = jnp.zeros((SUBLANE - MEM_HEADS, MEM_HDIM), F32)
    scores = []
    for j in range(nblk):
        q = jnp.concatenate([q_ref[j, :, hd * MEM_HDIM:(hd + 1) * MEM_HDIM] for hd in range(MEM_HEADS)] + [zero],
                            axis=0)
        scores.append(_dot_nt(q.astype(BF16), k_ref[j].astype(BF16)))
    for j in range(nblk):
        s = jnp.where(own, scores[j], NEG_BIG)
        p = jnp.exp(s - jnp.max(s, axis=-1, keepdims=True))
        p = jnp.where(own, p, 0.0)
        p = p / jnp.sum(p, axis=-1, keepdims=True)
        of = _dot(p.astype(BF16), v_ref[j].astype(BF16))
        o_ref[j] = jnp.concatenate([of[hd:hd + 1] for hd in range(MEM_HEADS)], axis=-1).astype(o_ref.dtype)


MEM_SAMPLE_BLOCK = 4


def _memattn_sample(q, k, v):
    n, m = k.shape[:2]
    sb = MEM_SAMPLE_BLOCK
    return pl.pallas_call(
        _memattn_sample_kernel,
        grid=(n // sb,),
        in_specs=[pl.BlockSpec((sb, 1, MEM_WIDTH), lambda i: (i, 0, 0)),
                  pl.BlockSpec((sb, m, MEM_HDIM), lambda i: (i, 0, 0)),
                  pl.BlockSpec((sb, m, MEM_HDIM), lambda i: (i, 0, 0))],
        out_specs=pl.BlockSpec((sb, 1, MEM_WIDTH), lambda i: (i, 0, 0)),
        out_shape=jax.ShapeDtypeStruct((n, 1, MEM_WIDTH), BF16),
        compiler_params=_params("parallel"),
        name="memattn_sample",
    )(q, k, v)


def _ffn_kernel(*refs, seq_mode, tiles_per_seq):
    if seq_mode:
        (x1_ref, om_ref, wmo_ref, g_ref, wup_ref, cw_ref, cb_ref, wdn_ref, gfin_ref,
         y_ref, tail_ref, carry_ref) = refs
    else:
        (x1_ref, om_ref, wmo_ref, g_ref, wup_ref, cw_ref, cb_ref, wdn_ref, gfin_ref, prev2_ref, prev1_ref,
         y_ref, tail_ref) = refs
    tm = x1_ref.shape[0]
    x2 = x1_ref[...] + _dot(om_ref[...], wmo_ref[...])
    h = _rms(x2, g_ref[...]).astype(BF16)
    acc = x2
    if seq_mode:
        first = pl.program_id(0) % tiles_per_seq == 0
        rid = lax.broadcasted_iota(jnp.int32, (tm, 1), 0)

        @pl.when(pl.program_id(0) == 0)
        def _():
            carry_ref[...] = jnp.zeros_like(carry_ref)
    ups = [(_dot(h, wup_ref[:, f0:f0 + FFN_CHUNK]), _dot(h, wup_ref[:, D_FF + f0:D_FF + f0 + FFN_CHUNK]))
           for f0 in range(0, D_FF, FFN_CHUNK)]
    for (a, gt), f0 in zip(ups, range(0, D_FF, FFN_CHUNK)):
        fs = slice(f0, f0 + FFN_CHUNK)
        if seq_mode:
            prev = jnp.where(first, 0.0, carry_ref[:, fs])
            p1 = prev[SUBLANE - 1:SUBLANE]
            p2 = prev[SUBLANE - 2:SUBLANE - 1]
            a1 = jnp.where(rid == 0, p1, pltpu.roll(a, 1, axis=0))
            a2 = jnp.where(rid == 0, p2, jnp.where(rid == 1, p1, pltpu.roll(a, 2, axis=0)))
            carry_ref[:, fs] = a[tm - SUBLANE:, :]
            tail_ref[:, fs] = a[tm - (CONV_W - 1):, :]
        else:
            a1 = prev1_ref[:, fs]
            a2 = prev2_ref[:, fs]
            tail_ref[:, fs] = a
        conv = cb_ref[:, fs] + a2 * cw_ref[0:1, fs] + a1 * cw_ref[1:2, fs] + a * cw_ref[2:3, fs]
        u = (_silu(conv) * gt).astype(BF16)
        acc = acc + _dot(u, wdn_ref[fs, :])
    y_ref[...] = _rms(acc, gfin_ref[...])


def _ffn(x1, om, w, tm, seq_len=None, prev=None):
    rows = x1.shape[0]
    seq_mode = prev is None
    row = lambda wd: pl.BlockSpec((tm, wd), lambda i: (i, 0))
    in_specs = [row(D_MODEL), row(MEM_WIDTH), _const_spec((MEM_WIDTH, D_MODEL)), _const_spec((1, D_MODEL)),
                _const_spec((D_MODEL, 2 * D_FF)), _const_spec((CONV_W, D_FF)), _const_spec((1, D_FF)),
                _const_spec((D_FF, D_MODEL)), _const_spec((1, D_MODEL))]
    args = [x1, om, w["w_mo"], w["norm_ffn"], w["w_up"], w["conv_w"], w["conv_b"], w["w_down"], w["norm_final"]]
    if seq_mode:
        tiles_per_seq = seq_len // tm
        n_seq = rows // seq_len
        tail_spec = pl.BlockSpec((None, CONV_W - 1, D_FF), lambda i: (i // tiles_per_seq, 0, 0))
        tail_shape = jax.ShapeDtypeStruct((n_seq, CONV_W - 1, D_FF), F32)
        scratch = [pltpu.VMEM((SUBLANE, D_FF), F32)]
        sem = "arbitrary"
    else:
        tiles_per_seq = 1
        in_specs += [row(D_FF), row(D_FF)]
        args += list(prev)
        tail_spec = row(D_FF)
        tail_shape = jax.ShapeDtypeStruct((rows, D_FF), F32)
        scratch = []
        sem = "parallel"
    return pl.pallas_call(
        functools.partial(_ffn_kernel, seq_mode=seq_mode, tiles_per_seq=tiles_per_seq),
        grid=(rows // tm,),
        in_specs=in_specs,
        out_specs=[row(D_MODEL), tail_spec],
        out_shape=[jax.ShapeDtypeStruct((rows, D_MODEL), F32), tail_shape],
        scratch_shapes=scratch,
        compiler_params=_params(sem),
        name="ffn_seq" if seq_mode else "ffn_step",
    )(*args)


def _rope_tables(pos):
    half = ROPE_HALF
    inv = ROPE_BASE ** (-jnp.arange(half, dtype=F32) / half)
    ang = pos[:, None] * inv[None, :]
    cos, sin = jnp.cos(ang), jnp.sin(ang)
    n = pos.shape[0]
    one = jnp.ones((n, MLA_NOPE), F32)
    zero = jnp.zeros((n, MLA_NOPE), F32)
    pad = jnp.zeros((n, LANE - MLA_NOPE - MLA_ROPE), F32)
    cos_t = jnp.concatenate([one, cos, cos, pad], axis=-1)
    sin_t = jnp.concatenate([zero, -sin, sin, pad], axis=-1)
    return cos_t, sin_t


def _rope_tile(w_rope):
    x1, x2 = w_rope[..., :ROPE_HALF], w_rope[..., ROPE_HALF:]
    return jnp.concatenate([x1, x2, x2, x1], axis=-1)


def _prep_weights(norm_mix, w_in, norm_q, norm_kv, w_uq, w_uk, w_uv, norm_hg, w_out, norm_memx, norm_mem,
                  w_mq, w_mk, w_mv, w_mo, norm_ffn, w_up, conv_w, conv_b, w_down, norm_final):
    off_kr = Q_LORA + KV_LORA
    off_hg = off_kr + MLA_ROPE
    kr_tile = jnp.concatenate([jnp.zeros((D_MODEL, MLA_NOPE), F32), _rope_tile(w_in[:, off_kr:off_hg])], axis=-1)
    w_in_pad = jnp.concatenate([w_in[:, :off_kr], kr_tile, w_in[:, off_hg:]], axis=-1)
    uq = jnp.concatenate([w_uq[..., :MLA_NOPE], _rope_tile(w_uq[..., MLA_NOPE:])], axis=-1)
    uk = jnp.concatenate([w_uk, jnp.zeros((KV_LORA, MLA_HEADS, HEAD_PAD - MLA_NOPE), F32)], axis=-1)
    ukt = jnp.transpose(uk, (1, 2, 0))
    row = lambda g: g.reshape(1, -1)
    return {
        "norm_mix": row(norm_mix), "w_in": w_in_pad.astype(BF16), "norm_q": row(norm_q), "norm_kv": row(norm_kv),
        "w_uq": uq.reshape(Q_LORA, MLA_HEADS * HEAD_PAD).astype(BF16),
        "w_uq_t": uq.reshape(Q_LORA, MLA_HEADS * HEAD_PAD).T.astype(BF16),
        "w_uv_t": w_uv.reshape(KV_LORA, MLA_WIDTH).T.astype(BF16),
        "w_uk": uk.reshape(KV_LORA, MLA_HEADS * HEAD_PAD).astype(BF16),
        "w_ukt": ukt.astype(BF16),
        "w_uv": w_uv.reshape(KV_LORA, MLA_WIDTH).astype(BF16),
        "norm_hg": row(norm_hg), "w_out": w_out.astype(BF16), "norm_memx": row(norm_memx),
        "norm_mem": row(norm_mem), "w_mq": w_mq.reshape(D_MODEL, MEM_WIDTH).astype(BF16),
        "w_mk": w_mk.reshape(D_MODEL, MEM_WIDTH).astype(BF16), "w_mv": w_mv.reshape(D_MODEL, MEM_WIDTH).astype(BF16),
        "w_mo": w_mo.reshape(MEM_WIDTH, D_MODEL).astype(BF16), "norm_ffn": row(norm_ffn),
        "w_up": w_up.astype(BF16), "conv_w": conv_w, "conv_b": row(conv_b), "w_down": w_down.astype(BF16),
        "norm_final": row(norm_final),
    }


def kernel(x_prompt, x_sample, mem_prompt, cache_ckv, cache_krope, page_table, cache_mem_k, cache_mem_v,
           state_hgrn, state_conv, norm_mix, w_in, norm_q, norm_kv, w_uq, w_uk, w_uv, hg_lb_raw, norm_hg,
           w_out, norm_memx, norm_mem, w_mq, w_mk, w_mv, w_mo, norm_ffn, w_up, conv_w, conv_b, w_down,
           norm_final):
    bp, t, _ = x_prompt.shape
    db = x_sample.shape[0]
    n_pages = page_table.shape[1]
    page = cache_ckv.shape[2]
    past_len = n_pages * page
    w = _prep_weights(norm_mix[0], w_in[0], norm_q[0], norm_kv[0], w_uq[0], w_uk[0], w_uv[0], norm_hg[0],
                      w_out[0], norm_memx[0], norm_mem[0], w_mq[0], w_mk[0], w_mv[0], w_mo[0], norm_ffn[0],
                      w_up[0], conv_w[0], conv_b[0], w_down[0], norm_final)
    tm = ROW_TILE
    rows_p = bp * t

    mk_p, mv_p, mk_pb, mv_pb = _memkv(mem_prompt.reshape(-1, D_MODEL), w["norm_mem"], w["w_mk"], w["w_mv"])
    cos_p, sin_p = _rope_tables(jnp.arange(t, dtype=F32))
    xp = x_prompt.reshape(rows_p, D_MODEL)
    qt_p, k_p, vt_p, ckv_p, kr_p, zh_p = _mix_in(xp, w, cos_p, sin_p, tm, True)
    hq = MLA_HEADS * HEAD_PAD
    o_mla_p = _mla_prompt(qt_p, k_p.reshape(bp, t, hq), vt_p, bp, t, tm)
    o_hg_p, s_p = _hgrn_prompt(zh_p.reshape(bp, t, -1), hg_lb_raw, w["norm_hg"])
    m_tok = mem_prompt.shape[1]
    x1_p, om_p = _outproj(xp, o_mla_p.reshape(rows_p, -1), o_hg_p.reshape(rows_p, -1), w, tm, BF16,
                          mem=(mk_pb.reshape(bp, m_tok, MEM_WIDTH), mv_pb.reshape(bp, m_tok, MEM_WIDTH)), seq_len=t)
    y_p, cv_p = _ffn(x1_p, om_p, w, tm, seq_len=t)

    cos_s, sin_s = _rope_tables(jnp.full((db,), past_len, F32))
    xs = x_sample.reshape(db, D_MODEL)
    q_s, _, _, ckv_s, kr_s, zh_s = _mix_in(xs, w, cos_s, sin_s, db, False)
    qlat = jnp.transpose(_qlat(q_s, w["w_ukt"]), (1, 0, 2))
    qrope = q_s.reshape(db, MLA_HEADS, HEAD_PAD)[:, :, MLA_NOPE:MLA_NOPE + MLA_ROPE].astype(F32)
    kr_new = kr_s[:, MLA_NOPE:MLA_NOPE + MLA_ROPE]
    lat = _paged_attention(page_table, qlat, qrope, ckv_s.reshape(db, 1, KV_LORA), kr_new.reshape(db, 1, MLA_ROPE),
                           cache_ckv.reshape(-1, page, KV_LORA),
                           jnp.swapaxes(cache_krope, 2, 3).reshape(-1, MLA_ROPE, page))
    o_mla_s = _latv(jnp.transpose(lat, (1, 0, 2)), w["w_uv"])
    s_s, o_hg_s = _hgrn_sample(state_hgrn.reshape(db, HG_HEADS, HG_DK, HG_DV), zh_s, hg_lb_raw, w["norm_hg"])
    x1_s, qm_s = _outproj(xs, o_mla_s, o_hg_s, w, db, F32)
    om_s = _memattn_sample(qm_s.reshape(db, 1, MEM_WIDTH), cache_mem_k.reshape(db, m_tok * MEM_HEADS, MEM_HDIM),
                           cache_mem_v.reshape(db, m_tok * MEM_HEADS, MEM_HDIM))
    y_s, a_s = _ffn(x1_s, om_s.reshape(db, MEM_WIDTH), w, db, prev=(state_conv[0, :, 0], state_conv[0, :, 1]))
    cv_s = jnp.stack([state_conv[0, :, 1], a_s], axis=1)

    rope_sl = slice(MLA_NOPE, MLA_NOPE + MLA_ROPE)
    return (y_p.reshape(bp, t, D_MODEL), y_s.reshape(db, 1, D_MODEL),
            ckv_p.reshape(1, bp, t, KV_LORA), kr_p[:, rope_sl].reshape(1, bp, t, MLA_ROPE),
            mk_p.reshape(1, bp, m_tok, MEM_HEADS, MEM_HDIM), mv_p.reshape(1, bp, m_tok, MEM_HEADS, MEM_HDIM),
            s_p[None], cv_p[None],
            ckv_s.reshape(1, db, 1, KV_LORA), kr_new.reshape(1, db, 1, MLA_ROPE),
            s_s[None], cv_s[None])
```
